```python
import math
import jax
import jax.numpy as jnp
from jax import lax
import numpy as np


D_MODEL = 1024
BATCH = 16
SEQ = 4096
DEPTH = 2

GRID_W = 64
CTX_LEN = 256
A_WIDTH = 512
A_GROUPS = 4
A_GROUP_DIM = A_WIDTH // A_GROUPS
A_CHUNK = 128
B_HEADS = 4
B_HEAD_DIM = 64
B_QK = B_HEADS * 2 * B_HEAD_DIM
B_V = B_HEADS * 2 * B_HEAD_DIM
Q_BLOCK = 128
ROPE_BASE = 10000.0
C_HEADS = 4
C_HEAD_DIM = 128
C_WIDTH = C_HEADS * C_HEAD_DIM
C_CONV = 3
GDN_CHUNK = 64
N_BRANCHES = 3
D_FF = 2816
FFN_CONV = 3
EPS = 1e-6
IN_SPLITS = (A_WIDTH, A_WIDTH, B_QK, B_QK, B_V, C_WIDTH, C_WIDTH, C_WIDTH, C_WIDTH,
             2 * C_HEADS, 2 * C_HEADS, N_BRANCHES * D_MODEL)
D_IN = sum(IN_SPLITS)

kernel_name = 'hybrid_gated_branch_diffusion_block'


def rms_norm(x, g):
    xf = x.astype(jnp.float32)
    y = xf * lax.rsqrt(jnp.mean(xf * xf, axis=-1, keepdims=True) + EPS)
    return (y * g.astype(jnp.float32)).astype(x.dtype)


def l2_normalize(x):
    return x * lax.rsqrt(jnp.sum(x * x, axis=-1, keepdims=True) + EPS)


def modulate(x, shift, scale):
    return x * (1 + scale) + shift


def dwconv_centred(x, w):
    pad = (w.shape[0] - 1) // 2
    return lax.conv_general_dilated(x, w[:, None, :].astype(x.dtype), window_strides=(1,),
                                    padding=[(pad, pad)], dimension_numbers=('NWC', 'WIO', 'NWC'),
                                    feature_group_count=x.shape[-1])


def split_in(p):
    idx = np.cumsum(IN_SPLITS)[:-1].tolist()
    return jnp.split(p, idx, axis=-1)


def axial_angles(rows):
    n_freq = B_HEAD_DIM // 4
    inv_freq = ROPE_BASE ** (-jnp.arange(n_freq, dtype=jnp.float32) / n_freq)
    row = jnp.repeat(jnp.arange(rows, dtype=jnp.float32), GRID_W)
    col = jnp.tile(jnp.arange(GRID_W, dtype=jnp.float32), rows)
    return row[:, None] * inv_freq, col[:, None] * inv_freq


def rope_half(x, ang):
    h = x.shape[-1] // 2
    cos = jnp.cos(ang).astype(x.dtype)
    sin = jnp.sin(ang).astype(x.dtype)
    x1, x2 = x[..., :h], x[..., h:]
    return jnp.concatenate([x1 * cos - x2 * sin, x2 * cos + x1 * sin], axis=-1)


def rope_axial(x, ang_r, ang_c):
    h = x.shape[-1] // 2
    return jnp.concatenate([rope_half(x[..., :h], ang_r), rope_half(x[..., h:], ang_c)], axis=-1)


def spatial_gating(u, v, norm_g, w_s, b_s):
    b, n, _ = v.shape
    u = jax.nn.gelu(u)
    v = rms_norm(jax.nn.gelu(v).reshape(b, n // A_CHUNK, A_CHUNK, A_GROUPS, A_GROUP_DIM), norm_g)
    mixed = jnp.einsum('gij,bnjgc->bnigc', w_s, v) + b_s.T[:, :, None]
    return u * mixed.reshape(b, n, A_WIDTH)


def diff_q(q, g):
    b, n, _ = q.shape
    return rms_norm(q.reshape(b, n, B_HEADS, 2, B_HEAD_DIM), g).transpose(0, 2, 3, 1, 4)


def diff_kv(k, v, g):
    b, n, _ = k.shape
    k = rms_norm(k.reshape(b, n, B_HEADS, 2, B_HEAD_DIM), g).transpose(0, 2, 3, 1, 4)
    v = v.reshape(b, n, B_HEADS, 2 * B_HEAD_DIM).transpose(0, 2, 1, 3)
    return k, v


def diff_softmax_combine(q, k, v, lam):
    s = jnp.einsum('bhmqd,bhmkd->bhmqk', q, k, preferred_element_type=jnp.float32) * B_HEAD_DIM ** -0.5
    p = jax.nn.softmax(s, axis=-1)
    a = p[:, :, 0] - lam * p[:, :, 1]
    return jnp.einsum('bhqk,bhkd->bhqd', a.astype(v.dtype), v)


def diff_attention_blocks(q, k_all, v_all, lam):
    b, h, _, n, hd = q.shape
    nblk = n // Q_BLOCK
    qb = q.reshape(b, h, 2, nblk, Q_BLOCK, hd).transpose(3, 0, 1, 2, 4, 5)
    o = lax.map(lambda qblk: diff_softmax_combine(qblk, k_all, v_all, lam), qb)
    return o.transpose(1, 2, 0, 3, 4).reshape(b, h, n, v_all.shape[-1])


def diff_output(o, subln_g, lam_init):
    b, h, n, dv = o.shape
    o = rms_norm(o, subln_g) * (1 - lam_init)
    return o.transpose(0, 2, 1, 3).reshape(b, n, h * dv)


def gdn_prepare(q, k, v, conv_w):
    b, n, _ = q.shape
    qkv = jax.nn.silu(dwconv_centred(jnp.concatenate([q, k, v], axis=-1), conv_w))
    heads = lambda t: t.reshape(b, n, C_HEADS, C_HEAD_DIM).transpose(0, 2, 1, 3).astype(jnp.float32)
    q, k, v = (heads(t) for t in jnp.split(qkv, 3, axis=-1))
    return l2_normalize(q), l2_normalize(k), v


def gdn_gates(beta_logits, a_logits, a_log, dt_bias):
    b, n, _ = beta_logits.shape
    per_dir = lambda t: t.astype(jnp.float32).reshape(b, n, 2, C_HEADS).transpose(2, 0, 3, 1)
    beta = jax.nn.sigmoid(per_dir(beta_logits))
    g = -jnp.exp(a_log.astype(jnp.float32))[:, None, :, None] * jax.nn.softplus(
        per_dir(a_logits) + dt_bias.astype(jnp.float32)[:, None, :, None])
    return beta, g


def gated_delta_chunked(q, k, v, g, beta, s0):
    b, h, n, dk = q.shape
    dv = v.shape[-1]
    nc = n // GDN_CHUNK
    chunks = lambda t: t.reshape(b, h, nc, GDN_CHUNK, *t.shape[3:])
    q = chunks(q * dk ** -0.5)
    k, v, g, beta = chunks(k), chunks(v), chunks(g), chunks(beta)
    G = jnp.cumsum(g, axis=-1)
    incl = jnp.tri(GDN_CHUNK, dtype=bool)
    strict = jnp.tri(GDN_CHUNK, k=-1, dtype=bool)
    seg = jnp.exp(jnp.where(incl, G[..., :, None] - G[..., None, :], -jnp.inf))
    kb = k * beta[..., None]
    a = jnp.where(strict, jnp.einsum('bhnid,bhnjd->bhnij', kb, k) * seg, 0.0)
    lhs = a + jnp.eye(GDN_CHUNK, dtype=q.dtype)
    u = lax.linalg.triangular_solve(lhs, v * beta[..., None], left_side=True, lower=True, unit_diagonal=True)
    w = lax.linalg.triangular_solve(lhs, kb * jnp.exp(G)[..., None], left_side=True, lower=True,
                                    unit_diagonal=True)
    intra = jnp.einsum('bhnid,bhnjd->bhnij', q, k) * seg
    q_dec = q * jnp.exp(G)[..., None]
    k_tail = k * jnp.exp(G[..., -1:] - G)[..., None]
    g_tot = jnp.exp(G[..., -1])

    def step(state, blk):
        u_c, w_c, qd_c, in_c, kt_c, gt_c = blk
        v_new = u_c - jnp.einsum('bhcd,bhde->bhce', w_c, state)
        o_c = jnp.einsum('bhcd,bhde->bhce', qd_c, state) + jnp.einsum('bhij,bhje->bhie', in_c, v_new)
        state = state * gt_c[..., None, None] + jnp.einsum('bhcd,bhce->bhde', kt_c, v_new)
        return state, o_c

    xs = tuple(jnp.moveaxis(t, 2, 0) for t in (u, w, q_dec, intra, k_tail, g_tot))
    s_fin, o = lax.scan(step, s0, xs)
    return jnp.moveaxis(o, 0, 2).reshape(b, h, n, dv), s_fin


def gdn_bidirectional(q_c, k_c, v_c, beta_c, g_c, q_x, k_x, v_x, beta_x, g_x):
    b, h, _, dk = q_c.shape
    s0 = jnp.zeros((b, h, dk, v_c.shape[-1]), jnp.float32)
    rev = lambda t: jnp.flip(t, axis=2)
    o_cf, s_cf = gated_delta_chunked(q_c, k_c, v_c, g_c[0], beta_c[0], s0)
    o_xf, _ = gated_delta_chunked(q_x, k_x, v_x, g_x[0], beta_x[0], s_cf)
    o_cb, s_cb = gated_delta_chunked(rev(q_c), rev(k_c), rev(v_c), rev(g_c[1]), rev(beta_c[1]), s0)
    o_xb, _ = gated_delta_chunked(rev(q_x), rev(k_x), rev(v_x), rev(g_x[1]), rev(beta_x[1]), s_cb)
    return o_cf + rev(o_cb), o_xf + rev(o_xb)


def gdn_output(o, z, norm_g):
    b, h, n, dv = o.shape
    o = rms_norm(o.transpose(0, 2, 1, 3), norm_g).astype(z.dtype)
    return (o * jax.nn.silu(z.reshape(b, n, h, dv))).reshape(b, n, h * dv)


def merge_branches(gate_logits, ya, yb, yc, w_a_br, w_b_br, w_c_br, w_o):
    b, n, _ = ya.shape
    g = jax.nn.sigmoid(gate_logits).reshape(b, n, N_BRANCHES, D_MODEL)
    y = g[:, :, 0] * (ya @ w_a_br) + g[:, :, 1] * (yb @ w_b_br) + g[:, :, 2] * (yc @ w_c_br)
    return y @ w_o


def conv_ffn(h, w_up, conv_w, w_down):
    gate, val = jnp.split(dwconv_centred(h @ w_up, conv_w), 2, axis=-1)
    return (jax.nn.silu(gate) * val) @ w_down


def hybrid_mixer(h_x, h_c, need_ctx, ang_r, ang_c, lam, lam_init, w_in, sgu_norm_g, sgu_w, sgu_b, w_a_br,
                 q_norm_g, k_norm_g, subln_g, w_b_br, conv_qkv_w, a_log, dt_bias, gdn_norm_g, w_c_br, w_o):
    (u_x, sv_x, bq_x, bk_x, bv_x, dq_x, dk_x, dv_x, dz_x, beta_x, a_x, gate_x) = split_in(h_x @ w_in)
    (u_c, sv_c, bq_c, bk_c, bv_c, dq_c, dk_c, dv_c, dz_c, beta_c, a_c, gate_c) = split_in(h_c @ w_in)
    ya_x = spatial_gating(u_x, sv_x, sgu_norm_g, sgu_w, sgu_b)
    q_x = rope_axial(diff_q(bq_x, q_norm_g), ang_r, ang_c)
    k_x, v_x = diff_kv(bk_x, bv_x, k_norm_g)
    k_x = rope_axial(k_x, ang_r, ang_c)
    k_c, v_c = diff_kv(bk_c, bv_c, k_norm_g)
    k_all = jnp.concatenate([k_c, k_x], axis=3)
    v_all = jnp.concatenate([v_c, v_x], axis=2)
    yb_x = diff_output(diff_attention_blocks(q_x, k_all, v_all, lam), subln_g, lam_init)
    gq_x, gk_x, gv_x = gdn_prepare(dq_x, dk_x, dv_x, conv_qkv_w)
    gq_c, gk_c, gv_c = gdn_prepare(dq_c, dk_c, dv_c, conv_qkv_w)
    be_x, g_x = gdn_gates(beta_x, a_x, a_log, dt_bias)
    be_c, g_c = gdn_gates(beta_c, a_c, a_log, dt_bias)
    o_c, o_x = gdn_bidirectional(gq_c, gk_c, gv_c, be_c, g_c, gq_x, gk_x, gv_x, be_x, g_x)
    yc_x = gdn_output(o_x, dz_x, gdn_norm_g)
    y_x = merge_branches(gate_x, ya_x, yb_x, yc_x, w_a_br, w_b_br, w_c_br, w_o)
    if not need_ctx:
        return y_x, None
    ya_c = spatial_gating(u_c, sv_c, sgu_norm_g, sgu_w, sgu_b)
    yb_c = diff_output(diff_softmax_combine(diff_q(bq_c, q_norm_g), k_c, v_c, lam), subln_g, lam_init)
    yc_c = gdn_output(o_c, dz_c, gdn_norm_g)
    y_c = merge_branches(gate_c, ya_c, yb_c, yc_c, w_a_br, w_b_br, w_c_br, w_o)
    return y_x, y_c


def setup_inputs(seed: int = 0) -> dict:
    key = jax.random.key(seed)
    ks = jax.random.split(key, 32)
    f32 = jnp.float32
    L = DEPTH
    nrm = lambda k, shape, scale: jax.random.normal(k, shape, f32) * scale
    gain = lambda k, shape: 1.0 + 0.1 * jax.random.normal(k, shape, f32)
    dt = jnp.exp(jax.random.uniform(ks[22], (L, 2, C_HEADS), f32, minval=math.log(1e-3), maxval=math.log(1e-1)))
    return {
        'x': nrm(ks[0], (BATCH, SEQ, D_MODEL), 1.0),
        'c': nrm(ks[1], (BATCH, D_MODEL), 1.0),
        'ctx': nrm(ks[2], (BATCH, CTX_LEN, D_MODEL), 1.0),
        'c_ctx': nrm(ks[3], (D_MODEL,), 1.0),
        'w_mod': nrm(ks[4], (L, D_MODEL, 6 * D_MODEL), 0.5 * D_MODEL ** -0.5),
        'b_mod': nrm(ks[5], (L, 6 * D_MODEL), 0.02),
        'norm1_g': gain(ks[6], (L, D_MODEL)),
        'w_in': nrm(ks[7], (L, D_MODEL, D_IN), D_MODEL ** -0.5),
        'sgu_norm_g': gain(ks[8], (L, A_GROUPS, A_GROUP_DIM)),
        'sgu_w': nrm(ks[9], (L, A_GROUPS, A_CHUNK, A_CHUNK), A_CHUNK ** -0.5),
        'sgu_b': gain(ks[10], (L, A_GROUPS, A_CHUNK)),
        'w_a_br': nrm(ks[11], (L, A_WIDTH, D_MODEL), A_WIDTH ** -0.5),
        'q_norm_g': gain(ks[12], (L, B_HEAD_DIM)),
        'k_norm_g': gain(ks[13], (L, B_HEAD_DIM)),
        'lambda_q1': nrm(ks[14], (L, B_HEAD_DIM), 0.1),
        'lambda_k1': nrm(ks[15], (L, B_HEAD_DIM), 0.1),
        'lambda_q2': nrm(ks[16], (L, B_HEAD_DIM), 0.1),
        'lambda_k2': nrm(ks[17], (L, B_HEAD_DIM), 0.1),
        'subln_g': gain(ks[18], (L, 2 * B_HEAD_DIM)),
        'w_b_br': nrm(ks[19], (L, B_V, D_MODEL), B_V ** -0.5),
        'conv_qkv_w': nrm(ks[20], (L, C_CONV, 3 * C_WIDTH), C_CONV ** -0.5),
        'a_log': jnp.log(jax.random.uniform(ks[21], (L, 2, C_HEADS), f32, minval=1.0, maxval=16.0)),
        'dt_bias': dt + jnp.log(-jnp.expm1(-dt)),
        'gdn_norm_g': gain(ks[23], (L, C_HEAD_DIM)),
        'w_c_br': nrm(ks[24], (L, C_WIDTH, D_MODEL), C_WIDTH ** -0.5),
        'w_o': nrm(ks[25], (L, D_MODEL, D_MODEL), D_MODEL ** -0.5),
        'norm2_g': gain(ks[26], (L, D_MODEL)),
        'w_up': nrm(ks[27], (L, D_MODEL, 2 * D_FF), D_MODEL ** -0.5),
        'conv_ffn_w': nrm(ks[28], (L, FFN_CONV, 2 * D_FF), FFN_CONV ** -0.5),
        'w_down': nrm(ks[29], (L, D_FF, D_MODEL), D_FF ** -0.5),
    }


def reference(x, c, ctx, c_ctx, w_mod, b_mod, norm1_g, w_in, sgu_norm_g, sgu_w, sgu_b, w_a_br, q_norm_g,
              k_norm_g, lambda_q1, lambda_k1, lambda_q2, lambda_k2, subln_g, w_b_br, conv_qkv_w, a_log,
              dt_bias, gdn_norm_g, w_c_br, w_o, norm2_g, w_up, conv_ffn_w, w_down):
    n = x.shape[1]
    rows = n // GRID_W
    ang_r, ang_c = axial_angles(rows)
    cx = ctx
    for l in range(DEPTH):
        last = l == DEPTH - 1
        lam_init = 0.8 - 0.6 * math.exp(-0.3 * l)
        lam = (jnp.exp(jnp.sum(lambda_q1[l].astype(jnp.float32) * lambda_k1[l].astype(jnp.float32)))
               - jnp.exp(jnp.sum(lambda_q2[l].astype(jnp.float32) * lambda_k2[l].astype(jnp.float32)))
               + lam_init)
        mod_x = jax.nn.silu(c) @ w_mod[l] + b_mod[l]
        mod_c = jax.nn.silu(c_ctx) @ w_mod[l] + b_mod[l]
        sh1_x, sc1_x, gt1_x, sh2_x, sc2_x, gt2_x = (m[:, None, :] for m in jnp.split(mod_x, 6, axis=-1))
        sh1_c, sc1_c, gt1_c, sh2_c, sc2_c, gt2_c = jnp.split(mod_c, 6, axis=-1)
        h_x = modulate(rms_norm(x, norm1_g[l]), sh1_x, sc1_x)
        h_c = modulate(rms_norm(cx, norm1_g[l]), sh1_c, sc1_c)
        y_x, y_c = hybrid_mixer(h_x, h_c, not last, ang_r, ang_c, lam, lam_init, w_in[l], sgu_norm_g[l],
                                sgu_w[l], sgu_b[l], w_a_br[l], q_norm_g[l], k_norm_g[l], subln_g[l], w_b_br[l],
                                conv_qkv_w[l], a_log[l], dt_bias[l], gdn_norm_g[l], w_c_br[l], w_o[l])
        x = x + gt1_x * y_x
        h_x = modulate(rms_norm(x, norm2_g[l]), sh2_x, sc2_x)
        x = x + gt2_x * conv_ffn(h_x, w_up[l], conv_ffn_w[l], w_down[l])
        if not last:
            cx = cx + gt1_c * y_c
            h_c = modulate(rms_norm(cx, norm2_g[l]), sh2_c, sc2_c)
            cx = cx + gt2_c * conv_ffn(h_c, w_up[l], conv_ffn_w[l], w_down[l])
    return x
```

```python
import functools
import math

import jax
import jax.numpy as jnp
from jax import lax
from jax.experimental import pallas as pl
from jax.experimental.pallas import tpu as pltpu

F32 = jnp.float32
BF16 = jnp.bfloat16
HIGHEST = lax.Precision.HIGHEST

GRID_W = 64
A_WIDTH = 512
A_GROUPS = 4
A_GROUP_DIM = A_WIDTH // A_GROUPS
A_CHUNK = 128
B_HEADS = 4
B_HEAD_DIM = 64
B_QK = B_HEADS * 2 * B_HEAD_DIM
B_V = B_HEADS * 2 * B_HEAD_DIM
ROPE_BASE = 10000.0
C_HEADS = 4
C_HEAD_DIM = 128
C_WIDTH = C_HEADS * C_HEAD_DIM
GDN_CHUNK = 64
N_BRANCHES = 3
EPS = 1e-6

LANES = 128
BF16_SUBLANES = 16
VMEM_LIMIT = 52 * 1024 * 1024

BLK = 512
COL_GATE = 0
COL_CQKV = 6
COL_CZ = 9
COL_U = 10
COL_SV = 11
COL_BQ = 12
COL_BK = 13
COL_BV = 14
N_MAIN = 15 * BLK
GATE_TAB = 128


def _cparams(sem):
    return pltpu.CompilerParams(dimension_semantics=sem, vmem_limit_bytes=VMEM_LIMIT)


def _mm(a, b):
    return jnp.dot(a.astype(BF16), b.astype(BF16), preferred_element_type=F32)


def _mm_nt(a, b):
    return lax.dot_general(a.astype(BF16), b.astype(BF16), (((1,), (1,)), ((), ())),
                           preferred_element_type=F32)


def _mm_tn(a, b):
    return lax.dot_general(a.astype(BF16), b.astype(BF16), (((0,), (0,)), ((), ())),
                           preferred_element_type=F32)


def _silu(x):
    return x * jax.nn.sigmoid(x)


def _mod_kernel(c_ref, w_ref, b_ref, o_ref):
    s = _silu(c_ref[...])
    o_ref[...] = jnp.dot(s, w_ref[...], preferred_element_type=F32, precision=HIGHEST) + b_ref[...]


def _modulation(cs, w, b):
    r, d = cs.shape
    n = w.shape[1]
    tn = 1536
    return pl.pallas_call(
        _mod_kernel,
        grid=(n // tn,),
        in_specs=[pl.BlockSpec((r, d), lambda j: (0, 0)),
                  pl.BlockSpec((d, tn), lambda j: (0, j)),
                  pl.BlockSpec((1, tn), lambda j: (0, j))],
        out_specs=pl.BlockSpec((r, tn), lambda j: (0, j)),
        out_shape=jax.ShapeDtypeStruct((r, n), F32),
        compiler_params=_cparams(("arbitrary",)),
        name="modulation",
    )(cs, w, b)


def _nmm_kernel(x_ref, g_ref, sh_ref, sc_ref, w_ref, o_ref, h_ref):
    @pl.when(pl.program_id(2) == 0)
    def _():
        x = x_ref[0]
        y = x * lax.rsqrt(jnp.mean(x * x, axis=-1, keepdims=True) + EPS) * g_ref[...]
        h_ref[...] = (y * (1.0 + sc_ref[0]) + sh_ref[0]).astype(BF16)

    o_ref[0] = jnp.dot(h_ref[...], w_ref[...], preferred_element_type=F32).astype(o_ref.dtype)


def _norm_mod_matmul(x, g, shift, scale, w, out_dtype, tm, tn, name):
    bsz, n, d = x.shape
    nc = w.shape[1]
    tm = min(tm, n)
    return pl.pallas_call(
        _nmm_kernel,
        grid=(bsz, n // tm, nc // tn),
        in_specs=[pl.BlockSpec((1, tm, d), lambda b, i, j: (b, i, 0)),
                  pl.BlockSpec((1, d), lambda b, i, j: (0, 0)),
                  pl.BlockSpec((1, 1, d), lambda b, i, j: (b, 0, 0)),
                  pl.BlockSpec((1, 1, d), lambda b, i, j: (b, 0, 0)),
                  pl.BlockSpec((d, tn), lambda b, i, j: (0, j))],
        out_specs=pl.BlockSpec((1, tm, tn), lambda b, i, j: (b, i, j)),
        out_shape=jax.ShapeDtypeStruct((bsz, n, nc), out_dtype),
        scratch_shapes=[pltpu.VMEM((tm, d), BF16)],
        compiler_params=_cparams(("parallel", "parallel", "arbitrary")),
        name=name,
    )(x, g, shift, scale, w)


def _sgu_kernel(u_ref, v_ref, ng_ref, w_ref, b_ref, o_ref):
    tm = u_ref.shape[1]
    for c in range(tm // A_CHUNK):
        rows = slice(c * A_CHUNK, (c + 1) * A_CHUNK)
        for g in range(A_GROUPS):
            cols = slice(g * A_GROUP_DIM, (g + 1) * A_GROUP_DIM)
            u = jax.nn.gelu(u_ref[0, rows, cols].astype(F32))
            v = jax.nn.gelu(v_ref[0, rows, cols].astype(F32))
            vn = v * lax.rsqrt(jnp.mean(v * v, axis=-1, keepdims=True) + EPS) * ng_ref[g:g + 1, :]
            mixed = _mm(w_ref[g], vn) + b_ref[:, g:g + 1]
            o_ref[0, rows, cols] = (u * mixed).astype(o_ref.dtype)


def _spatial_gating(p, ng, w, b_t, tm):
    bsz, n, _ = p.shape
    tm = min(tm, n)
    return pl.pallas_call(
        _sgu_kernel,
        grid=(bsz, n // tm),
        in_specs=[pl.BlockSpec((1, tm, BLK), lambda b, i: (b, i, COL_U)),
                  pl.BlockSpec((1, tm, BLK), lambda b, i: (b, i, COL_SV)),
                  pl.BlockSpec(ng.shape, lambda b, i: (0, 0)),
                  pl.BlockSpec(w.shape, lambda b, i: (0, 0, 0)),
                  pl.BlockSpec(b_t.shape, lambda b, i: (0, 0))],
        out_specs=pl.BlockSpec((1, tm, A_WIDTH), lambda b, i: (b, i, 0)),
        out_shape=jax.ShapeDtypeStruct((bsz, n, A_WIDTH), BF16),
        compiler_params=_cparams(("parallel", "parallel")),
        name="spatial_gating",
    )(p, p, ng, w, b_t)


def _seg_rms(x, g, m_ref):
    x2 = x * x
    hi = x2.astype(BF16)
    lo = (x2 - hi.astype(F32)).astype(BF16)
    ms = (jnp.dot(hi, m_ref[...], preferred_element_type=F32)
          + jnp.dot(lo, m_ref[...], preferred_element_type=F32))
    return x * lax.rsqrt(ms + EPS) * g


def _rope(x, cos, sin_signed):
    half = B_HEAD_DIM // 4
    width = x.shape[1]
    lane = lax.broadcasted_iota(jnp.int32, x.shape, 1)
    first = (lane & (2 * half - 1)) < half
    partner = jnp.where(first, pltpu.roll(x, width - half, axis=1), pltpu.roll(x, half, axis=1))
    return x * cos + partner * sin_signed


def _qk_prep_rope_kernel(q_ref, k_ref, qg_ref, kg_ref, m_ref, cos_ref, sin_ref, qo_ref, ko_ref):
    cos = cos_ref[...]
    sin = sin_ref[...]
    q = _rope(_seg_rms(q_ref[0].astype(F32), qg_ref[...], m_ref), cos, sin)
    k = _rope(_seg_rms(k_ref[0].astype(F32), kg_ref[...], m_ref), cos, sin)
    qo_ref[0] = (q * B_HEAD_DIM ** -0.5).astype(qo_ref.dtype)
    ko_ref[0] = k.astype(ko_ref.dtype)


def _qk_prep_plain_kernel(q_ref, k_ref, qg_ref, kg_ref, m_ref, qo_ref, ko_ref):
    q = _seg_rms(q_ref[0].astype(F32), qg_ref[...], m_ref)
    k = _seg_rms(k_ref[0].astype(F32), kg_ref[...], m_ref)
    qo_ref[0] = (q * B_HEAD_DIM ** -0.5).astype(qo_ref.dtype)
    ko_ref[0] = k.astype(ko_ref.dtype)


def _qk_prep(p, qg, kg, m_avg, tables, tm):
    bsz, n, _ = p.shape
    tm = min(tm, n)
    in_specs = [pl.BlockSpec((1, tm, BLK), lambda b, i: (b, i, COL_BQ)),
                pl.BlockSpec((1, tm, BLK), lambda b, i: (b, i, COL_BK)),
                pl.BlockSpec((1, B_QK), lambda b, i: (0, 0)),
                pl.BlockSpec((1, B_QK), lambda b, i: (0, 0)),
                pl.BlockSpec((B_QK, B_QK), lambda b, i: (0, 0))]
    args = [p, p, qg, kg, m_avg]
    if tables is None:
        body = _qk_prep_plain_kernel
    else:
        body = _qk_prep_rope_kernel
        in_specs += [pl.BlockSpec((tm, B_QK), lambda b, i: (i, 0)),
                     pl.BlockSpec((tm, B_QK), lambda b, i: (i, 0))]
        args += list(tables)
    out = jax.ShapeDtypeStruct((bsz, n, B_QK), BF16)
    return pl.pallas_call(
        body,
        grid=(bsz, n // tm),
        in_specs=in_specs,
        out_specs=[pl.BlockSpec((1, tm, B_QK), lambda b, i: (b, i, 0)),
                   pl.BlockSpec((1, tm, B_QK), lambda b, i: (b, i, 0))],
        out_shape=[out, out],
        compiler_params=_cparams(("parallel", "parallel")),
        name="qk_prep",
    )(*args)


def _attn_kernel(*refs, n_lat, lam_init):
    if n_lat:
        lam_ref, q_ref, kc_ref, vc_ref, kx_ref, vx_ref, g_ref, o_ref, qm_s, m_s, l_s, acc_s = refs
    else:
        lam_ref, q_ref, kc_ref, vc_ref, g_ref, o_ref, qm_s, m_s, l_s, acc_s = refs
    kb = pl.program_id(3)
    last = n_lat

    @pl.when(kb == 0)
    def _init():
        q = q_ref[0]
        lane = lax.broadcasted_iota(jnp.int32, q.shape, 1)
        zero = jnp.zeros_like(q)
        qm_s[0] = jnp.where(lane < B_HEAD_DIM, q, zero)
        qm_s[1] = jnp.where(lane >= B_HEAD_DIM, q, zero)
        m_s[...] = jnp.full(m_s.shape, -jnp.inf, F32)
        l_s[...] = jnp.zeros(l_s.shape, F32)
        acc_s[...] = jnp.zeros(acc_s.shape, F32)

    def attend(k, v):
        for m in range(2):
            s = lax.dot_general(qm_s[m], k, (((1,), (1,)), ((), ())), preferred_element_type=F32)
            m_prev = m_s[m]
            m_new = jnp.maximum(m_prev, jnp.max(s, axis=-1, keepdims=True))
            alpha = jnp.exp(m_prev - m_new)
            p = jnp.exp(s - m_new)
            l_s[m] = alpha * l_s[m] + jnp.sum(p, axis=-1, keepdims=True)
            acc_s[m] = alpha * acc_s[m] + jnp.dot(p.astype(BF16), v, preferred_element_type=F32)
            m_s[m] = m_new

    @pl.when(kb == 0)
    def _ctx():
        attend(kc_ref[0], vc_ref[0])

    if n_lat:
        @pl.when(kb > 0)
        def _lat():
            attend(kx_ref[0], vx_ref[0])

    @pl.when(kb == last)
    def _fin():
        lv = lam_ref[...]
        lam = (jnp.exp(jnp.sum(lv[0:1] * lv[1:2], axis=-1, keepdims=True))
               - jnp.exp(jnp.sum(lv[2:3] * lv[3:4], axis=-1, keepdims=True)) + lam_init)
        o = acc_s[0] / l_s[0] - lam * (acc_s[1] / l_s[1])
        o = o * lax.rsqrt(jnp.mean(o * o, axis=-1, keepdims=True) + EPS) * g_ref[...]
        o_ref[0] = (o * (1.0 - lam_init)).astype(o_ref.dtype)


def _diff_attention(lam_vecs, q, kc, pc, kx, px, subln_g, lam_init, tq, tk):
    bsz, n, _ = q.shape
    nctx = kc.shape[1]
    tq = min(tq, n)
    hv = B_V // B_HEADS
    vcol = COL_BV * (BLK // hv)
    if kx is not None:
        tk = min(tk, kx.shape[1])
    n_lat = 0 if kx is None else kx.shape[1] // tk
    in_specs = [pl.BlockSpec(lam_vecs.shape, lambda b, h, i, j: (0, 0)),
                pl.BlockSpec((1, tq, hv), lambda b, h, i, j: (b, i, h)),
                pl.BlockSpec((1, nctx, hv), lambda b, h, i, j: (b, 0, h)),
                pl.BlockSpec((1, nctx, hv), lambda b, h, i, j: (b, 0, vcol + h))]
    args = [lam_vecs, q, kc, pc]
    if n_lat:
        in_specs += [pl.BlockSpec((1, tk, hv), lambda b, h, i, j: (b, jnp.maximum(j - 1, 0), h)),
                     pl.BlockSpec((1, tk, hv), lambda b, h, i, j: (b, jnp.maximum(j - 1, 0), vcol + h))]
        args += [kx, px]
    in_specs.append(pl.BlockSpec((1, hv), lambda b, h, i, j: (0, 0)))
    args.append(subln_g)
    return pl.pallas_call(
        functools.partial(_attn_kernel, n_lat=n_lat, lam_init=lam_init),
        grid=(bsz, B_HEADS, n // tq, 1 + n_lat),
        in_specs=in_specs,
        out_specs=pl.BlockSpec((1, tq, hv), lambda b, h, i, j: (b, i, h)),
        out_shape=jax.ShapeDtypeStruct((bsz, n, B_V), BF16),
        scratch_shapes=[pltpu.VMEM((2, tq, hv), BF16),
                        pltpu.VMEM((2, tq, 1), F32),
                        pltpu.VMEM((2, tq, 1), F32),
                        pltpu.VMEM((2, tq, hv), F32)],
        compiler_params=_cparams(("parallel", "parallel", "parallel", "arbitrary")),
        name="diff_attention",
    )(*args)


def _conv3(cur, prev_row, next_row, w):
    t = cur.shape[0]
    rid = lax.broadcasted_iota(jnp.int32, cur.shape, 0)
    xm = jnp.where(rid == 0, prev_row, pltpu.roll(cur, 1, axis=0))
    xp = jnp.where(rid == t - 1, next_row, pltpu.roll(cur, t - 1, axis=0))
    return w[0:1] * xm + w[1:2] * cur + w[2:3] * xp


def _halo_rows(prev_ref, next_ref, i, n_tiles):
    h = prev_ref.shape[1]
    prev_row = prev_ref[0, h - 1:h, :].astype(F32)
    next_row = next_ref[0, 0:1, :].astype(F32)
    prev_row = jnp.where(i > 0, prev_row, jnp.zeros_like(prev_row))
    next_row = jnp.where(i < n_tiles - 1, next_row, jnp.zeros_like(next_row))
    return prev_row, next_row


def _halo_specs(tm, width, col, n):
    r = tm // BF16_SUBLANES
    last = n // BF16_SUBLANES - 1
    cur = pl.BlockSpec((1, tm, width), lambda b, i, *_: (b, i, col(*_)))
    prev = pl.BlockSpec((1, BF16_SUBLANES, width),
                        lambda b, i, *_: (b, jnp.maximum(i * r - 1, 0), col(*_)))
    nxt = pl.BlockSpec((1, BF16_SUBLANES, width),
                       lambda b, i, *_: (b, jnp.minimum((i + 1) * r, last), col(*_)))
    return cur, prev, nxt


def _gdn_prep_kernel(x_ref, xp_ref, xn_ref, cw_ref, lg_ref, alog_ref, dtb_ref,
                     q_ref, k_ref, v_ref, tab_ref, row_ref):
    i = pl.program_id(1)
    tm = x_ref.shape[1]
    prev_row, next_row = _halo_rows(xp_ref, xn_ref, i, pl.num_programs(1))
    y = _silu(_conv3(x_ref[0].astype(F32), prev_row, next_row, cw_ref[...]))
    for h in range(C_HEADS):
        cq = slice(h * C_HEAD_DIM, (h + 1) * C_HEAD_DIM)
        ck = slice(C_WIDTH + h * C_HEAD_DIM, C_WIDTH + (h + 1) * C_HEAD_DIM)
        q = y[:, cq]
        k = y[:, ck]
        qn = q * lax.rsqrt(jnp.sum(q * q, axis=-1, keepdims=True) + EPS)
        kn = k * lax.rsqrt(jnp.sum(k * k, axis=-1, keepdims=True) + EPS)
        q_ref[0, :, cq] = (qn * C_HEAD_DIM ** -0.5).astype(q_ref.dtype)
        k_ref[0, :, cq] = kn.astype(k_ref.dtype)
    v_ref[0] = y[:, 2 * C_WIDTH:].astype(v_ref.dtype)

    lg = lg_ref[0]
    col = lax.broadcasted_iota(jnp.int32, lg.shape, 1)
    beta = jax.nn.sigmoid(lg)
    z = lg + dtb_ref[...]
    softplus = jnp.maximum(z, 0.0) + jnp.log(1.0 + jnp.exp(-jnp.abs(z)))
    g = -jnp.exp(alog_ref[...]) * softplus
    g = jnp.where((col >= 2 * C_HEADS) & (col < 4 * C_HEADS), g, 0.0)
    ri = lax.broadcasted_iota(jnp.int32, (tm, tm), 0)
    ci = lax.broadcasted_iota(jnp.int32, (tm, tm), 1)
    same = lax.shift_right_logical(ri, 6) == lax.shift_right_logical(ci, 6)
    lower = jnp.where(same & (ci <= ri), 1.0, 0.0).astype(F32)
    upper = jnp.where(same & (ci >= ri), 1.0, 0.0).astype(F32)
    cum_f = jnp.dot(lower, g, preferred_element_type=F32, precision=HIGHEST)
    cum_b = jnp.dot(upper, g, preferred_element_type=F32, precision=HIGHEST)
    cum = jnp.where(col >= 3 * C_HEADS, cum_b, cum_f)
    tab = jnp.where(col < 2 * C_HEADS, beta, cum)
    tab_ref[0] = tab
    sel = (lax.broadcasted_iota(jnp.int32, (4 * C_HEADS, GATE_TAB), 0)
           == lax.broadcasted_iota(jnp.int32, (4 * C_HEADS, GATE_TAB), 1)).astype(F32)
    row_ref[0] = lax.dot_general(sel, tab, (((1,), (1,)), ((), ())), preferred_element_type=F32,
                                 precision=HIGHEST)


def _gdn_prep(p, logits, conv_w, alog_row, dtb_row, tm):
    bsz, n, _ = p.shape
    tm = min(tm, n)
    cur, prev, nxt = _halo_specs(tm, 3 * C_WIDTH, lambda: COL_CQKV // 3, n)
    bf = jax.ShapeDtypeStruct((bsz, n, C_WIDTH), BF16)
    tok = pl.BlockSpec((1, tm, C_WIDTH), lambda b, i: (b, i, 0))
    return pl.pallas_call(
        _gdn_prep_kernel,
        grid=(bsz, n // tm),
        in_specs=[cur, prev, nxt,
                  pl.BlockSpec(conv_w.shape, lambda b, i: (0, 0)),
                  pl.BlockSpec((1, tm, GATE_TAB), lambda b, i: (b, i, 0)),
                  pl.BlockSpec((1, GATE_TAB), lambda b, i: (0, 0)),
                  pl.BlockSpec((1, GATE_TAB), lambda b, i: (0, 0))],
        out_specs=[tok, tok, tok,
                   pl.BlockSpec((1, tm, GATE_TAB), lambda b, i: (b, i, 0)),
                   pl.BlockSpec((1, 4 * C_HEADS, tm), lambda b, i: (b, 0, i))],
        out_shape=[bf, bf, bf,
                   jax.ShapeDtypeStruct((bsz, n, GATE_TAB), F32),
                   jax.ShapeDtypeStruct((bsz, 4 * C_HEADS, n), F32)],
        compiler_params=_cparams(("parallel", "parallel")),
        name="gdn_prep",
    )(p, p, p, conv_w, logits, alog_row, dtb_row)


def _chunk_masks(d):
    ri = lax.broadcasted_iota(jnp.int32, (GDN_CHUNK, GDN_CHUNK), 0)
    ci = lax.broadcasted_iota(jnp.int32, (GDN_CHUNK, GDN_CHUNK), 1)
    if d == 0:
        return ri >= ci, ri > ci
    return ri <= ci, ri < ci


def _gate_views(tab, row, rows, d, h):
    beta = tab[rows, d * C_HEADS + h:d * C_HEADS + h + 1]
    gcol = tab[rows, 2 * C_HEADS + d * C_HEADS + h:2 * C_HEADS + d * C_HEADS + h + 1]
    grow = row[2 * C_HEADS + d * C_HEADS + h:2 * C_HEADS + d * C_HEADS + h + 1, rows]
    return beta, gcol, grow


def _gdn_local_kernel(k_ref, v_ref, tab_ref, row_ref, uf_ref, wf_ref, ub_ref, wb_ref):
    tl = k_ref.shape[1]
    tab = tab_ref[0]
    row = row_ref[0]
    nh = C_HEADS
    cw = nh * GDN_CHUNK
    r4 = lax.broadcasted_iota(jnp.int32, (cw, cw), 0)
    c4 = lax.broadcasted_iota(jnp.int32, (cw, cw), 1)
    bd_mask = lax.shift_right_logical(r4, 6) == lax.shift_right_logical(c4, 6)
    r64 = lax.broadcasted_iota(jnp.int32, (GDN_CHUNK, cw), 0)
    c64 = lax.broadcasted_iota(jnp.int32, (GDN_CHUNK, cw), 1) & (GDN_CHUNK - 1)
    eye_cat = r64 == c64

    def block_diag(b):
        return jnp.where(bd_mask, jnp.concatenate([b] * nh, axis=0), 0.0)

    def off_block(lg):
        shr = lax.shift_right_logical
        return (shr(r64, lg) != shr(c64, lg)) & (shr(r64, lg + 1) == shr(c64, lg + 1))

    for c in range(tl // GDN_CHUNK):
        rows = slice(c * GDN_CHUNK, (c + 1) * GDN_CHUNK)
        for d, (u_ref, w_ref) in enumerate(((uf_ref, wf_ref), (ub_ref, wb_ref))):
            incl, strict = _chunk_masks(d)
            a_list, rhs_list = [], []
            for h in range(nh):
                cols = slice(h * C_HEAD_DIM, (h + 1) * C_HEAD_DIM)
                k = k_ref[0, rows, cols].astype(F32)
                v = v_ref[0, rows, cols].astype(F32)
                beta, gcol, grow = _gate_views(tab, row, rows, d, h)
                seg = jnp.exp(jnp.minimum(gcol - grow, 0.0))
                kb = k * beta
                a_list.append(jnp.where(strict, _mm_nt(kb, k) * seg, 0.0))
                rhs_list.append(jnp.concatenate([v * beta, kb * jnp.exp(gcol)], axis=1))
            a = jnp.concatenate(a_list, axis=1)
            t = jnp.where(eye_cat, 1.0, 0.0) - jnp.where(off_block(0), a, 0.0)
            for lg in range(1, 6):
                x = _mm(jnp.where(off_block(lg), a, 0.0), block_diag(t))
                t = t - _mm(t, block_diag(x))
            for h in range(nh):
                cols = slice(h * C_HEAD_DIM, (h + 1) * C_HEAD_DIM)
                uw = _mm(t[:, h * GDN_CHUNK:(h + 1) * GDN_CHUNK], rhs_list[h])
                u_ref[0, rows, cols] = uw[:, :C_HEAD_DIM].astype(u_ref.dtype)
                w_ref[0, rows, cols] = uw[:, C_HEAD_DIM:].astype(w_ref.dtype)


def _gdn_local(k, v, tab, row, tl):
    bsz, n, _ = k.shape
    tl = min(tl, n)
    tok = pl.BlockSpec((1, tl, C_WIDTH), lambda b, i: (b, i, 0))
    out = jax.ShapeDtypeStruct((bsz, n, C_WIDTH), BF16)
    return pl.pallas_call(
        _gdn_local_kernel,
        grid=(bsz, n // tl),
        in_specs=[tok, tok,
                  pl.BlockSpec((1, tl, GATE_TAB), lambda b, i: (b, i, 0)),
                  pl.BlockSpec((1, 4 * C_HEADS, tl), lambda b, i: (b, 0, i))],
        out_specs=[tok, tok, tok, tok],
        out_shape=[out, out, out, out],
        compiler_params=_cparams(("parallel", "parallel")),
        name="gdn_local",
    )(k, v, tab, row)


SCAN_TILE = 2 * GDN_CHUNK


def _gdn_scan_kernel(qf_ref, kf_ref, uf_ref, wf_ref, tabf_ref, rowf_ref,
                     qb_ref, kb_ref, ub_ref, wb_ref, tabb_ref, rowb_ref, s0_ref,
                     of_ref, ob_ref, sfin_ref, s_ref):
    step = pl.program_id(1)

    @pl.when(step == 0)
    def _():
        s_ref[...] = s0_ref[0]

    dirs = ((qf_ref, kf_ref, uf_ref, wf_ref, tabf_ref, rowf_ref, of_ref),
            (qb_ref, kb_ref, ub_ref, wb_ref, tabb_ref, rowb_ref, ob_ref))
    n_chunks = SCAN_TILE // GDN_CHUNK
    for d, (q_ref, k_ref, u_ref, w_ref, tab_ref, row_ref, o_ref) in enumerate(dirs):
        incl, _ = _chunk_masks(d)
        tab = tab_ref[0]
        row = row_ref[0]
        order = range(n_chunks) if d == 0 else range(n_chunks - 1, -1, -1)
        for c in order:
            rows = slice(c * GDN_CHUNK, (c + 1) * GDN_CHUNK)
            for h in range(C_HEADS):
                cols = slice(h * C_HEAD_DIM, (h + 1) * C_HEAD_DIM)
                q = q_ref[0, rows, cols].astype(F32)
                k = k_ref[0, rows, cols].astype(F32)
                u = u_ref[0, rows, cols].astype(F32)
                w = w_ref[0, rows, cols]
                _, gcol, grow = _gate_views(tab, row, rows, d, h)
                glast = gcol[GDN_CHUNK - 1:GDN_CHUNK] if d == 0 else gcol[0:1]
                seg = jnp.where(incl, jnp.exp(jnp.minimum(gcol - grow, 0.0)), 0.0)
                intra = _mm_nt(q, k) * seg
                qd = q * jnp.exp(gcol)
                kt = k * jnp.exp(glast - gcol)
                s = s_ref[d, h]
                ws_qs = _mm(jnp.concatenate([w, qd.astype(BF16)], axis=0), s)
                v_new = u - ws_qs[:GDN_CHUNK]
                o_ref[0, rows, cols] = ws_qs[GDN_CHUNK:] + _mm(intra, v_new)
                s_ref[d, h] = s * jnp.exp(glast) + _mm_tn(kt, v_new)

    @pl.when(step == pl.num_programs(1) - 1)
    def _():
        sfin_ref[0] = s_ref[...]


def _gdn_scan(q, k, uf, wf, ub, wb, tab, row, s0):
    bsz, n, _ = q.shape
    nt = n // SCAN_TILE
    tok_f = pl.BlockSpec((1, SCAN_TILE, C_WIDTH), lambda b, i: (b, i, 0))
    tok_b = pl.BlockSpec((1, SCAN_TILE, C_WIDTH), lambda b, i: (b, nt - 1 - i, 0))
    tab_f = pl.BlockSpec((1, SCAN_TILE, GATE_TAB), lambda b, i: (b, i, 0))
    tab_b = pl.BlockSpec((1, SCAN_TILE, GATE_TAB), lambda b, i: (b, nt - 1 - i, 0))
    row_f = pl.BlockSpec((1, 4 * C_HEADS, SCAN_TILE), lambda b, i: (b, 0, i))
    row_b = pl.BlockSpec((1, 4 * C_HEADS, SCAN_TILE), lambda b, i: (b, 0, nt - 1 - i))
    st = pl.BlockSpec((1, 2, C_HEADS, C_HEAD_DIM, C_HEAD_DIM), lambda b, i: (b, 0, 0, 0, 0))
    o_sds = jax.ShapeDtypeStruct((bsz, n, C_WIDTH), F32)
    return pl.pallas_call(
        _gdn_scan_kernel,
        grid=(bsz, nt),
        in_specs=[tok_f, tok_f, tok_f, tok_f, tab_f, row_f,
                  tok_b, tok_b, tok_b, tok_b, tab_b, row_b, st],
        out_specs=[tok_f, tok_b, st],
        out_shape=[o_sds, o_sds, jax.ShapeDtypeStruct(s0.shape, F32)],
        scratch_shapes=[pltpu.VMEM((2, C_HEADS, C_HEAD_DIM, C_HEAD_DIM), F32)],
        compiler_params=_cparams(("parallel", "arbitrary")),
        name="gdn_scan",
    )(q, k, uf, wf, tab, row, q, k, ub, wb, tab, row, s0)


def _merge_kernel(x_ref, g0_ref, g1_ref, g2_ref, ya_ref, yb_ref, of_ref, ob_ref, z_ref, gn_ref,
                  gt_ref, wa_ref, wb_ref, wc_ref, wo_ref, o_ref):
    o = of_ref[0] + ob_ref[0]
    z = z_ref[0].astype(F32)
    parts = []
    for h in range(C_HEADS):
        cols = slice(h * C_HEAD_DIM, (h + 1) * C_HEAD_DIM)
        oh = o[:, cols]
        parts.append(oh * lax.rsqrt(jnp.mean(oh * oh, axis=-1, keepdims=True) + EPS))
    yc = jnp.concatenate(parts, axis=1) * gn_ref[...] * _silu(z)
    y = (jax.nn.sigmoid(g0_ref[0].astype(F32)) * jnp.dot(ya_ref[0], wa_ref[...], preferred_element_type=F32)
         + jax.nn.sigmoid(g1_ref[0].astype(F32)) * jnp.dot(yb_ref[0], wb_ref[...], preferred_element_type=F32)
         + jax.nn.sigmoid(g2_ref[0].astype(F32)) * _mm(yc, wc_ref[...]))
    o_ref[0] = x_ref[0] + gt_ref[0] * _mm(y, wo_ref[...])


def _merge(x, p, ya, yb, of, ob, gn, gt, wa, wb, wc, wo, tm):
    bsz, n, d = x.shape
    tm = min(tm, n)
    tok = lambda width, col: pl.BlockSpec((1, tm, width), lambda b, i: (b, i, col))
    full = lambda a: pl.BlockSpec(a.shape, lambda b, i: (0,) * a.ndim)
    return pl.pallas_call(
        _merge_kernel,
        grid=(bsz, n // tm),
        in_specs=[tok(d, 0), tok(d, 0), tok(d, 1), tok(d, 2),
                  tok(A_WIDTH, 0), tok(B_V, 0), tok(C_WIDTH, 0), tok(C_WIDTH, 0), tok(BLK, COL_CZ),
                  full(gn), pl.BlockSpec((1, 1, d), lambda b, i: (b, 0, 0)),
                  full(wa), full(wb), full(wc), full(wo)],
        out_specs=tok(d, 0),
        out_shape=jax.ShapeDtypeStruct(x.shape, F32),
        compiler_params=_cparams(("parallel", "parallel")),
        name="merge",
    )(x, p, p, p, ya, yb, of, ob, p, gn, gt, wa, wb, wc, wo)


def _ffn_down_kernel(x_ref, ug_ref, ugp_ref, ugn_ref, uv_ref, uvp_ref, uvn_ref, cwg_ref, cwv_ref,
                     wd_ref, gt_ref, o_ref, acc_ref):
    i = pl.program_id(1)
    kk = pl.program_id(2)
    nt = pl.num_programs(1)

    @pl.when(kk == 0)
    def _():
        acc_ref[...] = jnp.zeros(acc_ref.shape, F32)

    gp, gn = _halo_rows(ugp_ref, ugn_ref, i, nt)
    vp, vn = _halo_rows(uvp_ref, uvn_ref, i, nt)
    gate = _conv3(ug_ref[0].astype(F32), gp, gn, cwg_ref[...])
    val = _conv3(uv_ref[0].astype(F32), vp, vn, cwv_ref[...])
    acc_ref[...] += _mm(_silu(gate) * val, wd_ref[...])

    @pl.when(kk == pl.num_programs(2) - 1)
    def _():
        o_ref[0] = x_ref[0] + gt_ref[0] * acc_ref[...]


def _ffn_down(x, u, conv_w, wd, gt, tm, tk):
    bsz, n, d = x.shape
    f = wd.shape[0]
    tm = min(tm, n)
    nk = f // tk
    g_cur, g_prev, g_next = _halo_specs(tm, tk, lambda k: k, n)
    v_cur, v_prev, v_next = _halo_specs(tm, tk, lambda k: nk + k, n)
    return pl.pallas_call(
        _ffn_down_kernel,
        grid=(bsz, n // tm, nk),
        in_specs=[pl.BlockSpec((1, tm, d), lambda b, i, k: (b, i, 0)),
                  g_cur, g_prev, g_next, v_cur, v_prev, v_next,
                  pl.BlockSpec((3, tk), lambda b, i, k: (0, k)),
                  pl.BlockSpec((3, tk), lambda b, i, k: (0, nk + k)),
                  pl.BlockSpec((tk, d), lambda b, i, k: (k, 0)),
                  pl.BlockSpec((1, 1, d), lambda b, i, k: (b, 0, 0))],
        out_specs=pl.BlockSpec((1, tm, d), lambda b, i, k: (b, i, 0)),
        out_shape=jax.ShapeDtypeStruct(x.shape, F32),
        scratch_shapes=[pltpu.VMEM((tm, d), F32)],
        compiler_params=_cparams(("parallel", "parallel", "arbitrary")),
        name="ffn_down",
    )(x, u, u, u, u, u, u, conv_w, conv_w, wd, gt)


def _rope_tables(n):
    n_freq = B_HEAD_DIM // 4
    inv_freq = ROPE_BASE ** (-jnp.arange(n_freq, dtype=F32) / n_freq)
    pos = jnp.arange(n, dtype=jnp.int32)
    row = (pos // GRID_W).astype(F32)
    col = (pos % GRID_W).astype(F32)
    ang_r = row[:, None] * inv_freq
    ang_c = col[:, None] * inv_freq
    ang = jnp.concatenate([ang_r, ang_r, ang_c, ang_c], axis=-1)
    sign = jnp.tile(jnp.concatenate([-jnp.ones(n_freq, F32), jnp.ones(n_freq, F32)]), 2)
    reps = B_QK // B_HEAD_DIM
    return jnp.tile(jnp.cos(ang), (1, reps)), jnp.tile(jnp.sin(ang) * sign, (1, reps))


def _split_w_in(w):
    o_gate_logits = 2 * A_WIDTH + 2 * B_QK + B_V + 4 * C_WIDTH
    o_gates = o_gate_logits + 4 * C_HEADS
    o_c = 2 * A_WIDTH + 2 * B_QK + B_V
    main = jnp.concatenate([w[:, o_gates:], w[:, o_c:o_c + 4 * C_WIDTH], w[:, :o_c]], axis=1)
    logit = jnp.pad(w[:, o_gate_logits:o_gates], ((0, 0), (0, GATE_TAB - 4 * C_HEADS)))
    return main.astype(BF16), logit.astype(BF16)


def _gate_row(v):
    return jnp.pad(v.astype(F32).reshape(1, 2 * C_HEADS), ((0, 0), (2 * C_HEADS, GATE_TAB - 4 * C_HEADS)))


def _layer(l, last, x, cx, mod, prm, tables, m_avg):
    (norm1_g, w_in, sgu_norm_g, sgu_w, sgu_b, w_a_br, q_norm_g, k_norm_g, lam_vecs, subln_g, w_b_br,
     conv_qkv_w, a_log, dt_bias, gdn_norm_g, w_c_br, w_o, norm2_g, w_up, conv_ffn_w, w_down) = prm
    bsz, n, d = x.shape
    lam_init = 0.8 - 0.6 * math.exp(-0.3 * l)
    w_main, w_logit = _split_w_in(w_in)
    mod_x = mod[:bsz].reshape(bsz, 1, 6, d)
    mod_c = jnp.broadcast_to(mod[bsz:bsz + 1].reshape(1, 1, 6, d), (bsz, 1, 6, d))
    n1 = norm1_g.reshape(1, d)
    n2 = norm2_g.reshape(1, d)
    qg = jnp.tile(q_norm_g.astype(F32), B_QK // B_HEAD_DIM).reshape(1, B_QK)
    kg = jnp.tile(k_norm_g.astype(F32), B_QK // B_HEAD_DIM).reshape(1, B_QK)
    sub_g = subln_g.astype(F32).reshape(1, -1)
    gn = jnp.tile(gdn_norm_g.astype(F32), C_HEADS).reshape(1, C_WIDTH)
    alog_row, dtb_row = _gate_row(a_log), _gate_row(dt_bias)
    sgu_wb = sgu_w.astype(BF16)
    sgu_bt = sgu_b.astype(F32).T
    wa, wb, wc, wo = (t.astype(BF16) for t in (w_a_br, w_b_br, w_c_br, w_o))
    wu, wd = w_up.astype(BF16), w_down.astype(BF16)

    def project(t, m):
        p = _norm_mod_matmul(t, n1, m[:, :, 0], m[:, :, 1], w_main, BF16, 1024, 1536, "in_proj")
        lg = _norm_mod_matmul(t, n1, m[:, :, 0], m[:, :, 1], w_logit, F32, 1024, GATE_TAB, "in_proj_gates")
        return p, lg

    def gdn_pre(p, lg):
        q, k, v, tab, row = _gdn_prep(p, lg, conv_qkv_w.astype(F32), alog_row, dtb_row, 256)
        uf, wf, ub, wb_ = _gdn_local(k, v, tab, row, 128)
        return q, k, uf, wf, ub, wb_, tab, row

    p_x, lg_x = project(x, mod_x)
    p_c, lg_c = project(cx, mod_c)

    q_x, k_x = _qk_prep(p_x, qg, kg, m_avg, tables, 512)
    q_c, k_c = _qk_prep(p_c, qg, kg, m_avg, None, 512)
    yb_x = _diff_attention(lam_vecs, q_x, k_c, p_c, k_x, p_x, sub_g, lam_init, 1024, 1024)

    gx = gdn_pre(p_x, lg_x)
    gc = gdn_pre(p_c, lg_c)
    s0 = jnp.zeros((bsz, 2, C_HEADS, C_HEAD_DIM, C_HEAD_DIM), F32)
    of_c, ob_c, s_c = _gdn_scan(*gc, s0)
    of_x, ob_x, _ = _gdn_scan(*gx, s_c)

    def finish(t, p, yb, of, ob, m):
        ya = _spatial_gating(p, sgu_norm_g.astype(F32), sgu_wb, sgu_bt, 512)
        t = _merge(t, p, ya, yb, of, ob, gn, m[:, :, 2], wa, wb, wc, wo, 512)
        up = _norm_mod_matmul(t, n2, m[:, :, 3], m[:, :, 4], wu, BF16, 1024, 1408, "ffn_up")
        return _ffn_down(t, up, conv_ffn_w.astype(F32), wd, m[:, :, 5], 512, 1408)

    x = finish(x, p_x, yb_x, of_x, ob_x, mod_x)
    if not last:
        yb_c = _diff_attention(lam_vecs, q_c, k_c, p_c, None, None, sub_g, lam_init, 1024, 1024)
        cx = finish(cx, p_c, yb_c, of_c, ob_c, mod_c)
    return x, cx


def kernel(x, c, ctx, c_ctx, w_mod, b_mod, norm1_g, w_in, sgu_norm_g, sgu_w, sgu_b, w_a_br, q_norm_g, k_norm_g, lambda_q1, lambda_k1, lambda_q2, lambda_k2, subln_g, w_b_br, conv_qkv_w, a_log, dt_bias, gdn_norm_g, w_c_br, w_o, norm2_g, w_up, conv_ffn_w, w_down):
    bsz, n, d = x.shape
    depth = w_mod.shape[0]
    tables = _rope_tables(n)
    seg = jnp.arange(B_QK) // B_HEAD_DIM
    m_avg = jnp.where(seg[:, None] == seg[None, :], 1.0 / B_HEAD_DIM, 0.0).astype(BF16)
    rows = bsz + 1
    pad = (-rows) % 8
    cs = jnp.concatenate([c, c_ctx[None, :], jnp.zeros((pad, d), F32)], axis=0)
    cx = ctx
    for l in range(depth):
        mod = _modulation(cs, w_mod[l], b_mod[l].reshape(1, -1))
        lam_vecs = jnp.pad(jnp.stack([lambda_q1[l], lambda_k1[l], lambda_q2[l], lambda_k2[l]]).astype(F32),
                           ((0, 4), (0, LANES - B_HEAD_DIM)))
        prm = (norm1_g[l], w_in[l], sgu_norm_g[l], sgu_w[l], sgu_b[l], w_a_br[l], q_norm_g[l], k_norm_g[l],
               lam_vecs, subln_g[l], w_b_br[l], conv_qkv_w[l], a_log[l], dt_bias[l], gdn_norm_g[l],
               w_c_br[l], w_o[l], norm2_g[l], w_up[l], conv_ffn_w[l], w_down[l])
        x, cx = _layer(l, l == depth - 1, x, cx, mod, prm, tables, m_avg)
    return x
```

```python
import functools
import math

import jax
import jax.numpy as jnp
from jax import lax
from jax.experimental import pallas as pl
from jax.experimental.pallas import tpu as pltpu

F32 = jnp.float32
BF16 = jnp.bfloat16
HIGHEST = lax.Precision.HIGHEST

GRID_W = 64
A_WIDTH = 512
A_GROUPS = 4
A_GROUP_DIM = A_WIDTH // A_GROUPS
A_CHUNK = 128
B_HEADS = 4
B_HEAD_DIM = 64
B_QK = B_HEADS * 2 * B_HEAD_DIM
B_V = B_HEADS * 2 * B_HEAD_DIM
ROPE_BASE = 10000.0
C_HEADS = 4
C_HEAD_DIM = 128
C_WIDTH = C_HEADS * C_HEAD_DIM
GDN_CHUNK = 64
N_BRANCHES = 3
EPS = 1e-6

LANES = 128
BF16_SUBLANES = 16
VMEM_LIMIT = 52 * 1024 * 1024

BLK = 512
COL_GATE = 0
COL_CQKV = 6
COL_CZ = 9
COL_U = 10
COL_SV = 11
COL_BQ = 12
COL_BK = 13
COL_BV = 14
N_MAIN = 15 * BLK
GATE_TAB = 128


def _cparams(sem):
    return pltpu.CompilerParams(dimension_semantics=sem, vmem_limit_bytes=VMEM_LIMIT)


def _mm(a, b):
    return jnp.dot(a.astype(BF16), b.astype(BF16), preferred_element_type=F32)


def _mm_nt(a, b):
    return lax.dot_general(a.astype(BF16), b.astype(BF16), (((1,), (1,)), ((), ())),
                           preferred_element_type=F32)


def _mm_tn(a, b):
    return lax.dot_general(a.astype(BF16), b.astype(BF16), (((0,), (0,)), ((), ())),
                           preferred_element_type=F32)


def _silu(x):
    return x * jax.nn.sigmoid(x)


def _mod_kernel(c_ref, w_ref, b_ref, o_ref):
    s = _silu(c_ref[...])
    o_ref[...] = jnp.dot(s, w_ref[...], preferred_element_type=F32, precision=HIGHEST) + b_ref[...]


def _modulation(cs, w, b):
    r, d = cs.shape
    n = w.shape[1]
    tn = 1536
    return pl.pallas_call(
        _mod_kernel,
        grid=(n // tn,),
        in_specs=[pl.BlockSpec((r, d), lambda j: (0, 0)),
                  pl.BlockSpec((d, tn), lambda j: (0, j)),
                  pl.BlockSpec((1, tn), lambda j: (0, j))],
        out_specs=pl.BlockSpec((r, tn), lambda j: (0, j)),
        out_shape=jax.ShapeDtypeStruct((r, n), F32),
        compiler_params=_cparams(("arbitrary",)),
        name="modulation",
    )(cs, w, b)


def _nmm_kernel(x_ref, g_ref, sh_ref, sc_ref, w_ref, o_ref, h_ref):
    @pl.when(pl.program_id(2) == 0)
    def _():
        x = x_ref[0]
        y = x * lax.rsqrt(jnp.mean(x * x, axis=-1, keepdims=True) + EPS) * g_ref[...]
        h_ref[...] = (y * (1.0 + sc_ref[0]) + sh_ref[0]).astype(BF16)

    o_ref[0] = jnp.dot(h_ref[...], w_ref[...], preferred_element_type=F32).astype(o_ref.dtype)


def _norm_mod_matmul(x, g, shift, scale, w, out_dtype, tm, tn, name):
    bsz, n, d = x.shape
    nc = w.shape[1]
    tm = min(tm, n)
    return pl.pallas_call(
        _nmm_kernel,
        grid=(bsz, n // tm, nc // tn),
        in_specs=[pl.BlockSpec((1, tm, d), lambda b, i, j: (b, i, 0)),
                  pl.BlockSpec((1, d), lambda b, i, j: (0, 0)),
                  pl.BlockSpec((1, 1, d), lambda b, i, j: (b, 0, 0)),
                  pl.BlockSpec((1, 1, d), lambda b, i, j: (b, 0, 0)),
                  pl.BlockSpec((d, tn), lambda b, i, j: (0, j))],
        out_specs=pl.BlockSpec((1, tm, tn), lambda b, i, j: (b, i, j)),
        out_shape=jax.ShapeDtypeStruct((bsz, n, nc), out_dtype),
        scratch_shapes=[pltpu.VMEM((tm, d), BF16)],
        compiler_params=_cparams(("parallel", "parallel", "arbitrary")),
        name=name,
    )(x, g, shift, scale, w)


def _sgu_kernel(u_ref, v_ref, ng_ref, w_ref, b_ref, o_ref):
    tm = u_ref.shape[1]
    for c in range(tm // A_CHUNK):
        rows = slice(c * A_CHUNK, (c + 1) * A_CHUNK)
        for g in range(A_GROUPS):
            cols = slice(g * A_GROUP_DIM, (g + 1) * A_GROUP_DIM)
            u = jax.nn.gelu(u_ref[0, rows, cols].astype(F32))
            v = jax.nn.gelu(v_ref[0, rows, cols].astype(F32))
            vn = v * lax.rsqrt(jnp.mean(v * v, axis=-1, keepdims=True) + EPS) * ng_ref[g:g + 1, :]
            mixed = _mm(w_ref[g], vn) + b_ref[:, g:g + 1]
            o_ref[0, rows, cols] = (u * mixed).astype(o_ref.dtype)


def _spatial_gating(p, ng, w, b_t, tm):
    bsz, n, _ = p.shape
    tm = min(tm, n)
    return pl.pallas_call(
        _sgu_kernel,
        grid=(bsz, n // tm),
        in_specs=[pl.BlockSpec((1, tm, BLK), lambda b, i: (b, i, COL_U)),
                  pl.BlockSpec((1, tm, BLK), lambda b, i: (b, i, COL_SV)),
                  pl.BlockSpec(ng.shape, lambda b, i: (0, 0)),
                  pl.BlockSpec(w.shape, lambda b, i: (0, 0, 0)),
                  pl.BlockSpec(b_t.shape, lambda b, i: (0, 0))],
        out_specs=pl.BlockSpec((1, tm, A_WIDTH), lambda b, i: (b, i, 0)),
        out_shape=jax.ShapeDtypeStruct((bsz, n, A_WIDTH), BF16),
        compiler_params=_cparams(("parallel", "parallel")),
        name="spatial_gating",
    )(p, p, ng, w, b_t)


def _seg_rms(x, g, m_ref):
    x2 = x * x
    hi = x2.astype(BF16)
    lo = (x2 - hi.astype(F32)).astype(BF16)
    ms = (jnp.dot(hi, m_ref[...], preferred_element_type=F32)
          + jnp.dot(lo, m_ref[...], preferred_element_type=F32))
    return x * lax.rsqrt(ms + EPS) * g


def _rope(x, cos, sin_signed):
    half = B_HEAD_DIM // 4
    width = x.shape[1]
    lane = lax.broadcasted_iota(jnp.int32, x.shape, 1)
    first = (lane & (2 * half - 1)) < half
    partner = jnp.where(first, pltpu.roll(x, width - half, axis=1), pltpu.roll(x, half, axis=1))
    return x * cos + partner * sin_signed


def _transpose_store(x, eye_ref, o_ref):
    xt = lax.dot_general(eye_ref[...], x.astype(BF16), (((1,), (1,)), ((), ())), preferred_element_type=F32)
    o_ref[0] = xt.astype(o_ref.dtype)


Q_SCALE = B_HEAD_DIM ** -0.5 * math.log2(math.e)
ATTN_KEY_BLOCK = 1024


def _qk_prep_rope_kernel(q_ref, k_ref, qg_ref, kg_ref, m_ref, eye_ref, cos_ref, sin_ref, ko_ref, qt_ref):
    cos = cos_ref[...]
    sin = sin_ref[...]
    q = _rope(_seg_rms(q_ref[0].astype(F32), qg_ref[...], m_ref), cos, sin)
    k = _rope(_seg_rms(k_ref[0].astype(F32), kg_ref[...], m_ref), cos, sin)
    ko_ref[0] = k.astype(ko_ref.dtype)
    _transpose_store(q * Q_SCALE, eye_ref, qt_ref)


def _qk_prep_plain_kernel(q_ref, k_ref, qg_ref, kg_ref, m_ref, eye_ref, ko_ref, qt_ref):
    q = _seg_rms(q_ref[0].astype(F32), qg_ref[...], m_ref)
    k = _seg_rms(k_ref[0].astype(F32), kg_ref[...], m_ref)
    ko_ref[0] = k.astype(ko_ref.dtype)
    _transpose_store(q * Q_SCALE, eye_ref, qt_ref)


def _qk_prep(p, qg, kg, m_avg, eye, tables, tm):
    bsz, n, _ = p.shape
    tm = min(tm, n)
    in_specs = [pl.BlockSpec((1, tm, BLK), lambda b, i: (b, i, COL_BQ)),
                pl.BlockSpec((1, tm, BLK), lambda b, i: (b, i, COL_BK)),
                pl.BlockSpec((1, B_QK), lambda b, i: (0, 0)),
                pl.BlockSpec((1, B_QK), lambda b, i: (0, 0)),
                pl.BlockSpec((B_QK, B_QK), lambda b, i: (0, 0)),
                pl.BlockSpec((B_QK, B_QK), lambda b, i: (0, 0))]
    args = [p, p, qg, kg, m_avg, eye]
    if tables is None:
        body = _qk_prep_plain_kernel
    else:
        body = _qk_prep_rope_kernel
        in_specs += [pl.BlockSpec((tm, B_QK), lambda b, i: (i, 0)),
                     pl.BlockSpec((tm, B_QK), lambda b, i: (i, 0))]
        args += list(tables)
    return pl.pallas_call(
        body,
        grid=(bsz, n // tm),
        in_specs=in_specs,
        out_specs=[pl.BlockSpec((1, tm, B_QK), lambda b, i: (b, i, 0)),
                   pl.BlockSpec((1, B_QK, tm), lambda b, i: (b, 0, i))],
        out_shape=[jax.ShapeDtypeStruct((bsz, n, B_QK), BF16),
                   jax.ShapeDtypeStruct((bsz, B_QK, n), BF16)],
        compiler_params=_cparams(("parallel", "parallel")),
        name="qk_prep",
    )(*args)


def _attn_kernel(*refs, has_lat, lam_init):
    if has_lat:
        lam_ref, qt_ref, kc_ref, vc_ref, kx_ref, vx_ref, g_ref, eye_ref, o_ref, vtc_s, vtx_s = refs
    else:
        lam_ref, qt_ref, kc_ref, vc_ref, g_ref, eye_ref, o_ref, vtc_s = refs
    hv = vc_ref.shape[2]

    @pl.when(pl.program_id(2) == 0)
    def _():
        pairs = ((vc_ref, vtc_s), (vx_ref, vtx_s)) if has_lat else ((vc_ref, vtc_s),)
        for v_ref, vt_s in pairs:
            vt = lax.dot_general(eye_ref[...], v_ref[0], (((1,), (1,)), ((), ())), preferred_element_type=F32)
            vt_s[:hv, :] = vt.astype(BF16)
            vt_s[hv:, :] = jnp.ones((hv, vt_s.shape[1]), BF16)

    qt = qt_ref[0]
    row = lax.broadcasted_iota(jnp.int32, qt.shape, 0)
    zero = jnp.zeros_like(qt)
    n_blk = kx_ref.shape[1] // ATTN_KEY_BLOCK if has_lat else 0
    blk = lambda j: slice((j - 1) * ATTN_KEY_BLOCK, j * ATTN_KEY_BLOCK)
    keys = lambda j: kc_ref[0] if j == 0 else kx_ref[0, blk(j), :]
    vals_t = lambda j: vtc_s[...] if j == 0 else vtx_s[:, blk(j)]
    q_sub = [jnp.where(row < B_HEAD_DIM, qt, zero), jnp.where(row >= B_HEAD_DIM, qt, zero)]
    score = lambda m, j: jnp.dot(keys(j), q_sub[m], preferred_element_type=F32)
    prob = lambda s, mx: jnp.exp2(s - mx).astype(BF16)
    pv = lambda j, p: jnp.dot(vals_t(j), p, preferred_element_type=F32)

    def col_max(blocks):
        mx = jnp.max(blocks[0], axis=0, keepdims=True)
        for s in blocks[1:]:
            mx = jnp.maximum(mx, jnp.max(s, axis=0, keepdims=True))
        return mx

    blocks = range(n_blk + 1)
    s0 = [score(0, j) for j in blocks]
    mx0 = col_max(s0)
    p0, s1 = [], []
    for j in blocks:
        p0.append(prob(s0[j], mx0))
        s1.append(score(1, j))
    mx1 = col_max(s1)
    acc0, p1 = None, []
    for j in blocks:
        p1.append(prob(s1[j], mx1))
        t = pv(j, p0[j])
        acc0 = t if acc0 is None else acc0 + t
    acc1 = None
    for j in blocks:
        t = pv(j, p1[j])
        acc1 = t if acc1 is None else acc1 + t
    outs = [acc[:hv] / acc[hv:] for acc in (acc0, acc1)]
    lv = lam_ref[...]
    lam = (jnp.exp(jnp.sum(lv[0:1] * lv[1:2], axis=-1, keepdims=True))
           - jnp.exp(jnp.sum(lv[2:3] * lv[3:4], axis=-1, keepdims=True)) + lam_init)
    o = outs[0] - lam * outs[1]
    o = o * lax.rsqrt(jnp.mean(o * o, axis=0, keepdims=True) + EPS) * g_ref[...]
    o_ref[0] = (o * (1.0 - lam_init)).T.astype(o_ref.dtype)


def _diff_attention(lam_vecs, qt, kc, pc, kx, px, subln_g, eye, lam_init, tq):
    bsz, _, n = qt.shape
    nctx = kc.shape[1]
    tq = min(tq, n)
    hv = B_V // B_HEADS
    vcol = COL_BV * (BLK // hv)
    has_lat = kx is not None
    in_specs = [pl.BlockSpec(lam_vecs.shape, lambda b, h, i: (0, 0)),
                pl.BlockSpec((1, hv, tq), lambda b, h, i: (b, h, i)),
                pl.BlockSpec((1, nctx, hv), lambda b, h, i: (b, 0, h)),
                pl.BlockSpec((1, nctx, hv), lambda b, h, i: (b, 0, vcol + h))]
    args = [lam_vecs, qt, kc, pc]
    scratch = [pltpu.VMEM((2 * hv, nctx), BF16)]
    if has_lat:
        nlat = kx.shape[1]
        in_specs += [pl.BlockSpec((1, nlat, hv), lambda b, h, i: (b, 0, h)),
                     pl.BlockSpec((1, nlat, hv), lambda b, h, i: (b, 0, vcol + h))]
        args += [kx, px]
        scratch.append(pltpu.VMEM((2 * hv, nlat), BF16))
    in_specs += [pl.BlockSpec((hv, 1), lambda b, h, i: (0, 0)),
                 pl.BlockSpec((hv, hv), lambda b, h, i: (0, 0))]
    args += [subln_g, eye]
    return pl.pallas_call(
        functools.partial(_attn_kernel, has_lat=has_lat, lam_init=lam_init),
        grid=(bsz, B_HEADS, n // tq),
        in_specs=in_specs,
        out_specs=pl.BlockSpec((1, tq, hv), lambda b, h, i: (b, i, h)),
        out_shape=jax.ShapeDtypeStruct((bsz, n, B_V), BF16),
        scratch_shapes=scratch,
        compiler_params=_cparams(("parallel", "parallel", "arbitrary")),
        name="diff_attention",
    )(*args)


def _conv3(cur, prev_row, next_row, w):
    t = cur.shape[0]
    rid = lax.broadcasted_iota(jnp.int32, cur.shape, 0)
    xm = jnp.where(rid == 0, prev_row, pltpu.roll(cur, 1, axis=0))
    xp = jnp.where(rid == t - 1, next_row, pltpu.roll(cur, t - 1, axis=0))
    return w[0:1] * xm + w[1:2] * cur + w[2:3] * xp


def _halo_rows(prev_ref, next_ref, i, n_tiles):
    h = prev_ref.shape[1]
    prev_row = prev_ref[0, h - 1:h, :].astype(F32)
    next_row = next_ref[0, 0:1, :].astype(F32)
    prev_row = jnp.where(i > 0, prev_row, jnp.zeros_like(prev_row))
    next_row = jnp.where(i < n_tiles - 1, next_row, jnp.zeros_like(next_row))
    return prev_row, next_row


def _halo_specs(tm, width, col, n):
    r = tm // BF16_SUBLANES
    last = n // BF16_SUBLANES - 1
    cur = pl.BlockSpec((1, tm, width), lambda b, i, *_: (b, i, col(*_)))
    prev = pl.BlockSpec((1, BF16_SUBLANES, width),
                        lambda b, i, *_: (b, jnp.maximum(i * r - 1, 0), col(*_)))
    nxt = pl.BlockSpec((1, BF16_SUBLANES, width),
                       lambda b, i, *_: (b, jnp.minimum((i + 1) * r, last), col(*_)))
    return cur, prev, nxt


def _gdn_prep_kernel(x_ref, xp_ref, xn_ref, cw_ref, lg_ref, alog_ref, dtb_ref, eye_ref,
                     q_ref, k_ref, kt_ref, v_ref, tab_ref, row_ref):
    i = pl.program_id(1)
    tm = x_ref.shape[1]
    prev_row, next_row = _halo_rows(xp_ref, xn_ref, i, pl.num_programs(1))
    y = _silu(_conv3(x_ref[0].astype(F32), prev_row, next_row, cw_ref[...]))
    for h in range(C_HEADS):
        cq = slice(h * C_HEAD_DIM, (h + 1) * C_HEAD_DIM)
        ck = slice(C_WIDTH + h * C_HEAD_DIM, C_WIDTH + (h + 1) * C_HEAD_DIM)
        q = y[:, cq]
        k = y[:, ck]
        qn = q * lax.rsqrt(jnp.sum(q * q, axis=-1, keepdims=True) + EPS)
        kn = k * lax.rsqrt(jnp.sum(k * k, axis=-1, keepdims=True) + EPS)
        q_ref[0, :, cq] = (qn * C_HEAD_DIM ** -0.5).astype(q_ref.dtype)
        k_ref[0, :, cq] = kn.astype(k_ref.dtype)
    _transpose_store(k_ref[0], eye_ref, kt_ref)
    v_ref[0] = y[:, 2 * C_WIDTH:].astype(v_ref.dtype)

    lg = lg_ref[0]
    col = lax.broadcasted_iota(jnp.int32, lg.shape, 1)
    beta = jax.nn.sigmoid(lg)
    z = lg + dtb_ref[...]
    softplus = jnp.maximum(z, 0.0) + jnp.log(1.0 + jnp.exp(-jnp.abs(z)))
    g = -jnp.exp(alog_ref[...]) * softplus
    g = jnp.where((col >= 2 * C_HEADS) & (col < 4 * C_HEADS), g, 0.0)
    ri = lax.broadcasted_iota(jnp.int32, (tm, tm), 0)
    ci = lax.broadcasted_iota(jnp.int32, (tm, tm), 1)
    same = lax.shift_right_logical(ri, 6) == lax.shift_right_logical(ci, 6)
    lower = jnp.where(same & (ci <= ri), 1.0, 0.0).astype(F32)
    upper = jnp.where(same & (ci >= ri), 1.0, 0.0).astype(F32)
    cum_f = jnp.dot(lower, g, preferred_element_type=F32, precision=HIGHEST)
    cum_b = jnp.dot(upper, g, preferred_element_type=F32, precision=HIGHEST)
    cum = jnp.where(col >= 3 * C_HEADS, cum_b, cum_f)
    tab = jnp.where(col < 2 * C_HEADS, beta, cum)
    tab_ref[0] = tab
    sel = (lax.broadcasted_iota(jnp.int32, (4 * C_HEADS, GATE_TAB), 0)
           == lax.broadcasted_iota(jnp.int32, (4 * C_HEADS, GATE_TAB), 1)).astype(F32)
    row_ref[0] = lax.dot_general(sel, tab, (((1,), (1,)), ((), ())), preferred_element_type=F32,
                                 precision=HIGHEST)


def _gdn_prep(p, logits, conv_w, alog_row, dtb_row, eye, tm):
    bsz, n, _ = p.shape
    tm = min(tm, n)
    cur, prev, nxt = _halo_specs(tm, 3 * C_WIDTH, lambda: COL_CQKV // 3, n)
    bf = jax.ShapeDtypeStruct((bsz, n, C_WIDTH), BF16)
    tok = pl.BlockSpec((1, tm, C_WIDTH), lambda b, i: (b, i, 0))
    return pl.pallas_call(
        _gdn_prep_kernel,
        grid=(bsz, n // tm),
        in_specs=[cur, prev, nxt,
                  pl.BlockSpec(conv_w.shape, lambda b, i: (0, 0)),
                  pl.BlockSpec((1, tm, GATE_TAB), lambda b, i: (b, i, 0)),
                  pl.BlockSpec((1, GATE_TAB), lambda b, i: (0, 0)),
                  pl.BlockSpec((1, GATE_TAB), lambda b, i: (0, 0)),
                  pl.BlockSpec(eye.shape, lambda b, i: (0, 0))],
        out_specs=[tok, tok,
                   pl.BlockSpec((1, C_WIDTH, tm), lambda b, i: (b, 0, i)),
                   tok,
                   pl.BlockSpec((1, tm, GATE_TAB), lambda b, i: (b, i, 0)),
                   pl.BlockSpec((1, 4 * C_HEADS, tm), lambda b, i: (b, 0, i))],
        out_shape=[bf, bf, jax.ShapeDtypeStruct((bsz, C_WIDTH, n), BF16), bf,
                   jax.ShapeDtypeStruct((bsz, n, GATE_TAB), F32),
                   jax.ShapeDtypeStruct((bsz, 4 * C_HEADS, n), F32)],
        compiler_params=_cparams(("parallel", "parallel")),
        name="gdn_prep",
    )(p, p, p, conv_w, logits, alog_row, dtb_row, eye)


def _chunk_masks(d):
    ri = lax.broadcasted_iota(jnp.int32, (GDN_CHUNK, GDN_CHUNK), 0)
    ci = lax.broadcasted_iota(jnp.int32, (GDN_CHUNK, GDN_CHUNK), 1)
    if d == 0:
        return ri >= ci, ri > ci
    return ri <= ci, ri < ci


def _gate_views(tab, row, rows, d, h):
    beta = tab[rows, d * C_HEADS + h:d * C_HEADS + h + 1]
    gcol = tab[rows, 2 * C_HEADS + d * C_HEADS + h:2 * C_HEADS + d * C_HEADS + h + 1]
    grow = row[2 * C_HEADS + d * C_HEADS + h:2 * C_HEADS + d * C_HEADS + h + 1, rows]
    return beta, gcol, grow


def _gdn_local_kernel(k_ref, kt_ref, v_ref, tab_ref, row_ref, uf_ref, wf_ref, ub_ref, wb_ref):
    tl = k_ref.shape[1]
    tab = tab_ref[0]
    row = row_ref[0]
    nh = C_HEADS
    cw = nh * GDN_CHUNK
    r4 = lax.broadcasted_iota(jnp.int32, (cw, cw), 0)
    c4 = lax.broadcasted_iota(jnp.int32, (cw, cw), 1)
    bd_mask = lax.shift_right_logical(r4, 6) == lax.shift_right_logical(c4, 6)
    r64 = lax.broadcasted_iota(jnp.int32, (GDN_CHUNK, cw), 0)
    c64 = lax.broadcasted_iota(jnp.int32, (GDN_CHUNK, cw), 1) & (GDN_CHUNK - 1)
    eye_cat = jnp.where(r64 == c64, 1.0, 0.0)
    bd_zero = jnp.zeros((cw, cw), BF16)

    def block_diag(b):
        return jnp.where(bd_mask, jnp.concatenate([b.astype(BF16)] * nh, axis=0), bd_zero)

    def off_block(lg):
        shr = lax.shift_right_logical
        return (shr(r64, lg) != shr(c64, lg)) & (shr(r64, lg + 1) == shr(c64, lg + 1))

    out_refs = ((uf_ref, wf_ref), (ub_ref, wb_ref))
    units = [(c, d) for c in range(tl // GDN_CHUNK) for d in range(2)]
    a_cat, rhs = [], []
    for c, d in units:
        rows = slice(c * GDN_CHUNK, (c + 1) * GDN_CHUNK)
        _, strict = _chunk_masks(d)
        a_list, rhs_list = [], []
        for h in range(nh):
            cols = slice(h * C_HEAD_DIM, (h + 1) * C_HEAD_DIM)
            k = k_ref[0, rows, cols].astype(F32)
            v = v_ref[0, rows, cols].astype(F32)
            beta, gcol, grow = _gate_views(tab, row, rows, d, h)
            seg = jnp.exp(jnp.minimum(gcol - grow, 0.0))
            kb = k * beta
            a_list.append(jnp.where(strict, _mm(kb, kt_ref[0, cols, rows]) * seg, 0.0))
            rhs_list.append(jnp.concatenate([v * beta, kb * jnp.exp(gcol)], axis=1).astype(BF16))
        a_cat.append(jnp.concatenate(a_list, axis=1))
        rhs.append(rhs_list)
    t = [eye_cat - jnp.where(off_block(0), a, 0.0) for a in a_cat]
    for lg in range(1, 6):
        x = [_mm(jnp.where(off_block(lg), a, 0.0), block_diag(tu)) for a, tu in zip(a_cat, t)]
        t = [tu - _mm(tu, block_diag(xu)) for tu, xu in zip(t, x)]
    for (c, d), tu, rhs_list in zip(units, t, rhs):
        rows = slice(c * GDN_CHUNK, (c + 1) * GDN_CHUNK)
        u_ref, w_ref = out_refs[d]
        for h in range(nh):
            cols = slice(h * C_HEAD_DIM, (h + 1) * C_HEAD_DIM)
            uw = _mm(tu[:, h * GDN_CHUNK:(h + 1) * GDN_CHUNK], rhs_list[h])
            u_ref[0, rows, cols] = uw[:, :C_HEAD_DIM].astype(u_ref.dtype)
            w_ref[0, rows, cols] = uw[:, C_HEAD_DIM:].astype(w_ref.dtype)


def _gdn_local(k, kt, v, tab, row, tl):
    bsz, n, _ = k.shape
    tl = min(tl, n)
    tok = pl.BlockSpec((1, tl, C_WIDTH), lambda b, i: (b, i, 0))
    out = jax.ShapeDtypeStruct((bsz, n, C_WIDTH), BF16)
    return pl.pallas_call(
        _gdn_local_kernel,
        grid=(bsz, n // tl),
        in_specs=[tok,
                  pl.BlockSpec((1, C_WIDTH, tl), lambda b, i: (b, 0, i)),
                  tok,
                  pl.BlockSpec((1, tl, GATE_TAB), lambda b, i: (b, i, 0)),
                  pl.BlockSpec((1, 4 * C_HEADS, tl), lambda b, i: (b, 0, i))],
        out_specs=[tok, tok, tok, tok],
        out_shape=[out, out, out, out],
        compiler_params=_cparams(("parallel", "parallel")),
        name="gdn_local",
    )(k, kt, v, tab, row)


SCAN_TILE = 2 * GDN_CHUNK


def _gdn_scan_kernel(qf_ref, ktf_ref, uf_ref, wf_ref, tabf_ref, rowf_ref,
                     qb_ref, ktb_ref, ub_ref, wb_ref, tabb_ref, rowb_ref, s0_ref,
                     of_ref, ob_ref, sfin_ref, s_ref):
    step = pl.program_id(1)

    @pl.when(step == 0)
    def _():
        s_ref[...] = s0_ref[0]

    dirs = ((qf_ref, ktf_ref, uf_ref, wf_ref, tabf_ref, rowf_ref, of_ref),
            (qb_ref, ktb_ref, ub_ref, wb_ref, tabb_ref, rowb_ref, ob_ref))
    n_chunks = SCAN_TILE // GDN_CHUNK
    chains = [(d, h) for d in range(2) for h in range(C_HEADS)]
    tabs = [r[4][0] for r in dirs]
    rowtabs = [r[5][0] for r in dirs]
    for sub in range(n_chunks):
        local = []
        for d, h in chains:
            q_ref, kt_ref, u_ref, w_ref, _, _, _ = dirs[d]
            c = sub if d == 0 else n_chunks - 1 - sub
            rows = slice(c * GDN_CHUNK, (c + 1) * GDN_CHUNK)
            cols = slice(h * C_HEAD_DIM, (h + 1) * C_HEAD_DIM)
            incl, _ = _chunk_masks(d)
            q = q_ref[0, rows, cols].astype(F32)
            kt = kt_ref[0, cols, rows].astype(F32)
            _, gcol, grow = _gate_views(tabs[d], rowtabs[d], rows, d, h)
            glast = gcol[GDN_CHUNK - 1:GDN_CHUNK] if d == 0 else gcol[0:1]
            seg = jnp.where(incl, jnp.exp(jnp.minimum(gcol - grow, 0.0)), 0.0)
            intra = (_mm(q, kt) * seg).astype(BF16)
            qd = (q * jnp.exp(gcol)).astype(BF16)
            kt_tail = (kt * jnp.exp(glast - grow)).astype(BF16)
            lhs = jnp.concatenate([w_ref[0, rows, cols], qd], axis=0)
            local.append((rows, cols, lhs, intra, kt_tail, jnp.exp(glast)))
        states = [s_ref[d, h] for d, h in chains]
        ws_qs = [_mm(loc[2], s) for loc, s in zip(local, states)]
        for (d, h), loc, s, wq in zip(chains, local, states, ws_qs):
            rows, cols, _, intra, kt_tail, gtot = loc
            u_ref, o_ref = dirs[d][2], dirs[d][6]
            v_new = (u_ref[0, rows, cols].astype(F32) - wq[:GDN_CHUNK]).astype(BF16)
            o_ref[0, rows, cols] = wq[GDN_CHUNK:] + jnp.dot(intra, v_new, preferred_element_type=F32)
            s_ref[d, h] = s * gtot + jnp.dot(kt_tail, v_new, preferred_element_type=F32)

    @pl.when(step == pl.num_programs(1) - 1)
    def _():
        sfin_ref[0] = s_ref[...]


def _gdn_scan(q, kt, uf, wf, ub, wb, tab, row, s0):
    bsz, n, _ = q.shape
    nt = n // SCAN_TILE
    tok_f = pl.BlockSpec((1, SCAN_TILE, C_WIDTH), lambda b, i: (b, i, 0))
    tok_b = pl.BlockSpec((1, SCAN_TILE, C_WIDTH), lambda b, i: (b, nt - 1 - i, 0))
    ktr_f = pl.BlockSpec((1, C_WIDTH, SCAN_TILE), lambda b, i: (b, 0, i))
    ktr_b = pl.BlockSpec((1, C_WIDTH, SCAN_TILE), lambda b, i: (b, 0, nt - 1 - i))
    tab_f = pl.BlockSpec((1, SCAN_TILE, GATE_TAB), lambda b, i: (b, i, 0))
    tab_b = pl.BlockSpec((1, SCAN_TILE, GATE_TAB), lambda b, i: (b, nt - 1 - i, 0))
    row_f = pl.BlockSpec((1, 4 * C_HEADS, SCAN_TILE), lambda b, i: (b, 0, i))
    row_b = pl.BlockSpec((1, 4 * C_HEADS, SCAN_TILE), lambda b, i: (b, 0, nt - 1 - i))
    st = pl.BlockSpec((1, 2, C_HEADS, C_HEAD_DIM, C_HEAD_DIM), lambda b, i: (b, 0, 0, 0, 0))
    o_sds = jax.ShapeDtypeStruct((bsz, n, C_WIDTH), F32)
    return pl.pallas_call(
        _gdn_scan_kernel,
        grid=(bsz, nt),
        in_specs=[tok_f, ktr_f, tok_f, tok_f, tab_f, row_f,
                  tok_b, ktr_b, tok_b, tok_b, tab_b, row_b, st],
        out_specs=[tok_f, tok_b, st],
        out_shape=[o_sds, o_sds, jax.ShapeDtypeStruct(s0.shape, F32)],
        scratch_shapes=[pltpu.VMEM((2, C_HEADS, C_HEAD_DIM, C_HEAD_DIM), F32)],
        compiler_params=_cparams(("parallel", "arbitrary")),
        name="gdn_scan",
    )(q, kt, uf, wf, tab, row, q, kt, ub, wb, tab, row, s0)


def _merge_kernel(x_ref, g0_ref, g1_ref, g2_ref, ya_ref, yb_ref, of_ref, ob_ref, z_ref, gn_ref,
                  gt_ref, wa_ref, wb_ref, wc_ref, wo_ref, o_ref):
    o = of_ref[0] + ob_ref[0]
    z = z_ref[0].astype(F32)
    parts = []
    for h in range(C_HEADS):
        cols = slice(h * C_HEAD_DIM, (h + 1) * C_HEAD_DIM)
        oh = o[:, cols]
        parts.append(oh * lax.rsqrt(jnp.mean(oh * oh, axis=-1, keepdims=True) + EPS))
    yc = jnp.concatenate(parts, axis=1) * gn_ref[...] * _silu(z)
    y = (jax.nn.sigmoid(g0_ref[0].astype(F32)) * jnp.dot(ya_ref[0], wa_ref[...], preferred_element_type=F32)
         + jax.nn.sigmoid(g1_ref[0].astype(F32)) * jnp.dot(yb_ref[0], wb_ref[...], preferred_element_type=F32)
         + jax.nn.sigmoid(g2_ref[0].astype(F32)) * _mm(yc, wc_ref[...]))
    o_ref[0] = x_ref[0] + gt_ref[0] * _mm(y, wo_ref[...])


def _merge(x, p, ya, yb, of, ob, gn, gt, wa, wb, wc, wo, tm):
    bsz, n, d = x.shape
    tm = min(tm, n)
    tok = lambda width, col: pl.BlockSpec((1, tm, width), lambda b, i: (b, i, col))
    full = lambda a: pl.BlockSpec(a.shape, lambda b, i: (0,) * a.ndim)
    return pl.pallas_call(
        _merge_kernel,
        grid=(bsz, n // tm),
        in_specs=[tok(d, 0), tok(d, 0), tok(d, 1), tok(d, 2),
                  tok(A_WIDTH, 0), tok(B_V, 0), tok(C_WIDTH, 0), tok(C_WIDTH, 0), tok(BLK, COL_CZ),
                  full(gn), pl.BlockSpec((1, 1, d), lambda b, i: (b, 0, 0)),
                  full(wa), full(wb), full(wc), full(wo)],
        out_specs=tok(d, 0),
        out_shape=jax.ShapeDtypeStruct(x.shape, F32),
        compiler_params=_cparams(("parallel", "parallel")),
        name="merge",
    )(x, p, p, p, ya, yb, of, ob, p, gn, gt, wa, wb, wc, wo)


def _ffn_down_kernel(x_ref, ug_ref, ugp_ref, ugn_ref, uv_ref, uvp_ref, uvn_ref, cwg_ref, cwv_ref,
                     wd_ref, gt_ref, o_ref, acc_ref):
    i = pl.program_id(1)
    kk = pl.program_id(2)
    nt = pl.num_programs(1)

    @pl.when(kk == 0)
    def _():
        acc_ref[...] = jnp.zeros(acc_ref.shape, F32)

    gp, gn = _halo_rows(ugp_ref, ugn_ref, i, nt)
    vp, vn = _halo_rows(uvp_ref, uvn_ref, i, nt)
    gate = _conv3(ug_ref[0].astype(F32), gp, gn, cwg_ref[...])
    val = _conv3(uv_ref[0].astype(F32), vp, vn, cwv_ref[...])
    acc_ref[...] += _mm(_silu(gate) * val, wd_ref[...])

    @pl.when(kk == pl.num_programs(2) - 1)
    def _():
        o_ref[0] = x_ref[0] + gt_ref[0] * acc_ref[...]


def _ffn_down(x, u, conv_w, wd, gt, tm, tk):
    bsz, n, d = x.shape
    f = wd.shape[0]
    tm = min(tm, n)
    nk = f // tk
    g_cur, g_prev, g_next = _halo_specs(tm, tk, lambda k: k, n)
    v_cur, v_prev, v_next = _halo_specs(tm, tk, lambda k: nk + k, n)
    return pl.pallas_call(
        _ffn_down_kernel,
        grid=(bsz, n // tm, nk),
        in_specs=[pl.BlockSpec((1, tm, d), lambda b, i, k: (b, i, 0)),
                  g_cur, g_prev, g_next, v_cur, v_prev, v_next,
                  pl.BlockSpec((3, tk), lambda b, i, k: (0, k)),
                  pl.BlockSpec((3, tk), lambda b, i, k: (0, nk + k)),
                  pl.BlockSpec((tk, d), lambda b, i, k: (k, 0)),
                  pl.BlockSpec((1, 1, d), lambda b, i, k: (b, 0, 0))],
        out_specs=pl.BlockSpec((1, tm, d), lambda b, i, k: (b, i, 0)),
        out_shape=jax.ShapeDtypeStruct(x.shape, F32),
        scratch_shapes=[pltpu.VMEM((tm, d), F32)],
        compiler_params=_cparams(("parallel", "parallel", "arbitrary")),
        name="ffn_down",
    )(x, u, u, u, u, u, u, conv_w, conv_w, wd, gt)


def _rope_tables(n):
    n_freq = B_HEAD_DIM // 4
    inv_freq = ROPE_BASE ** (-jnp.arange(n_freq, dtype=F32) / n_freq)
    pos = jnp.arange(n, dtype=jnp.int32)
    row = (pos // GRID_W).astype(F32)
    col = (pos % GRID_W).astype(F32)
    ang_r = row[:, None] * inv_freq
    ang_c = col[:, None] * inv_freq
    ang = jnp.concatenate([ang_r, ang_r, ang_c, ang_c], axis=-1)
    sign = jnp.tile(jnp.concatenate([-jnp.ones(n_freq, F32), jnp.ones(n_freq, F32)]), 2)
    reps = B_QK // B_HEAD_DIM
    return jnp.tile(jnp.cos(ang), (1, reps)), jnp.tile(jnp.sin(ang) * sign, (1, reps))


def _split_w_in(w):
    o_gate_logits = 2 * A_WIDTH + 2 * B_QK + B_V + 4 * C_WIDTH
    o_gates = o_gate_logits + 4 * C_HEADS
    o_c = 2 * A_WIDTH + 2 * B_QK + B_V
    main = jnp.concatenate([w[:, o_gates:], w[:, o_c:o_c + 4 * C_WIDTH], w[:, :o_c]], axis=1)
    logit = jnp.pad(w[:, o_gate_logits:o_gates], ((0, 0), (0, GATE_TAB - 4 * C_HEADS)))
    return main.astype(BF16), logit.astype(BF16)


def _gate_row(v):
    return jnp.pad(v.astype(F32).reshape(1, 2 * C_HEADS), ((0, 0), (2 * C_HEADS, GATE_TAB - 4 * C_HEADS)))


def _layer(l, last, x, cx, mod, prm, tables, m_avg, eye_qk):
    (norm1_g, w_in, sgu_norm_g, sgu_w, sgu_b, w_a_br, q_norm_g, k_norm_g, lam_vecs, subln_g, w_b_br,
     conv_qkv_w, a_log, dt_bias, gdn_norm_g, w_c_br, w_o, norm2_g, w_up, conv_ffn_w, w_down) = prm
    bsz, n, d = x.shape
    lam_init = 0.8 - 0.6 * math.exp(-0.3 * l)
    w_main, w_logit = _split_w_in(w_in)
    mod_x = mod[:bsz].reshape(bsz, 1, 6, d)
    mod_c = jnp.broadcast_to(mod[bsz:bsz + 1].reshape(1, 1, 6, d), (bsz, 1, 6, d))
    n1 = norm1_g.reshape(1, d)
    n2 = norm2_g.reshape(1, d)
    qg = jnp.tile(q_norm_g.astype(F32), B_QK // B_HEAD_DIM).reshape(1, B_QK)
    kg = jnp.tile(k_norm_g.astype(F32), B_QK // B_HEAD_DIM).reshape(1, B_QK)
    sub_g = subln_g.astype(F32).reshape(-1, 1)
    gn = jnp.tile(gdn_norm_g.astype(F32), C_HEADS).reshape(1, C_WIDTH)
    alog_row, dtb_row = _gate_row(a_log), _gate_row(dt_bias)
    sgu_wb = sgu_w.astype(BF16)
    sgu_bt = sgu_b.astype(F32).T
    wa, wb, wc, wo = (t.astype(BF16) for t in (w_a_br, w_b_br, w_c_br, w_o))
    wu, wd = w_up.astype(BF16), w_down.astype(BF16)

    def project(t, m):
        p = _norm_mod_matmul(t, n1, m[:, :, 0], m[:, :, 1], w_main, BF16, 1024, 1536, "in_proj")
        lg = _norm_mod_matmul(t, n1, m[:, :, 0], m[:, :, 1], w_logit, F32, 1024, GATE_TAB, "in_proj_gates")
        return p, lg

    def gdn_pre(p, lg):
        q, k, kt, v, tab, row = _gdn_prep(p, lg, conv_qkv_w.astype(F32), alog_row, dtb_row, eye_qk, 256)
        uf, wf, ub, wb_ = _gdn_local(k, kt, v, tab, row, 256)
        return q, kt, uf, wf, ub, wb_, tab, row

    p_x, lg_x = project(x, mod_x)
    p_c, lg_c = project(cx, mod_c)

    eye_v = eye_qk[:B_V // B_HEADS, :B_V // B_HEADS]
    k_x, qt_x = _qk_prep(p_x, qg, kg, m_avg, eye_qk, tables, 512)
    k_c, qt_c = _qk_prep(p_c, qg, kg, m_avg, eye_qk, None, 512)
    yb_x = _diff_attention(lam_vecs, qt_x, k_c, p_c, k_x, p_x, sub_g, eye_v, lam_init, 512)

    gx = gdn_pre(p_x, lg_x)
    gc = gdn_pre(p_c, lg_c)
    s0 = jnp.zeros((bsz, 2, C_HEADS, C_HEAD_DIM, C_HEAD_DIM), F32)
    of_c, ob_c, s_c = _gdn_scan(*gc, s0)
    of_x, ob_x, _ = _gdn_scan(*gx, s_c)

    def finish(t, p, yb, of, ob, m):
        ya = _spatial_gating(p, sgu_norm_g.astype(F32), sgu_wb, sgu_bt, 512)
        t = _merge(t, p, ya, yb, of, ob, gn, m[:, :, 2], wa, wb, wc, wo, 512)
        up = _norm_mod_matmul(t, n2, m[:, :, 3], m[:, :, 4], wu, BF16, 1024, 1408, "ffn_up")
        return _ffn_down(t, up, conv_ffn_w.astype(F32), wd, m[:, :, 5], 512, 1408)

    x = finish(x, p_x, yb_x, of_x, ob_x, mod_x)
    if not last:
        yb_c = _diff_attention(lam_vecs, qt_c, k_c, p_c, None, None, sub_g, eye_v, lam_init, 512)
        cx = finish(cx, p_c, yb_c, of_c, ob_c, mod_c)
    return x, cx


def kernel(x, c, ctx, c_ctx, w_mod, b_mod, norm1_g, w_in, sgu_norm_g, sgu_w, sgu_b, w_a_br, q_norm_g, k_norm_g, lambda_q1, lambda_k1, lambda_q2, lambda_k2, subln_g, w_b_br, conv_qkv_w, a_log, dt_bias, gdn_norm_g, w_c_br, w_o, norm2_g, w_up, conv_ffn_w, w_down):
    bsz, n, d = x.shape
    depth = w_mod.shape[0]
    tables = _rope_tables(n)
    seg = jnp.arange(B_QK) // B_HEAD_DIM
    m_avg = jnp.where(seg[:, None] == seg[None, :], 1.0 / B_HEAD_DIM, 0.0).astype(BF16)
    eye_qk = jnp.eye(B_QK, dtype=BF16)
    rows = bsz + 1
    pad = (-rows) % 8
    cs = jnp.concatenate([c, c_ctx[None, :], jnp.zeros((pad, d), F32)], axis=0)
    cx = ctx
    for l in range(depth):
        mod = _modulation(cs, w_mod[l], b_mod[l].reshape(1, -1))
        lam_vecs = jnp.pad(jnp.stack([lambda_q1[l], lambda_k1[l], lambda_q2[l], lambda_k2[l]]).astype(F32),
                           ((0, 4), (0, LANES - B_HEAD_DIM)))
        prm = (norm1_g[l], w_in[l], sgu_norm_g[l], sgu_w[l], sgu_b[l], w_a_br[l], q_norm_g[l], k_norm_g[l],
               lam_vecs, subln_g[l], w_b_br[l], conv_qkv_w[l], a_log[l], dt_bias[l], gdn_norm_g[l],
               w_c_br[l], w_o[l], norm2_g[l], w_up[l], conv_ffn_w[l], w_down[l])
        x, cx = _layer(l, l == depth - 1, x, cx, mod, prm, tables, m_avg, eye_qk)
    return x
```

```python
import functools
import math

import jax
import jax.numpy as jnp
from jax import lax
from jax.experimental import pallas as pl
from jax.experimental.pallas import tpu as pltpu

F32 = jnp.float32
BF16 = jnp.bfloat16
HIGHEST = lax.Precision.HIGHEST

GRID_W = 64
A_WIDTH = 512
A_GROUPS = 4
A_GROUP_DIM = A_WIDTH // A_GROUPS
A_CHUNK = 128
B_HEADS = 4
B_HEAD_DIM = 64
B_QK = B_HEADS * 2 * B_HEAD_DIM
B_V = B_HEADS * 2 * B_HEAD_DIM
ROPE_BASE = 10000.0
C_HEADS = 4
C_HEAD_DIM = 128
C_WIDTH = C_HEADS * C_HEAD_DIM
GDN_CHUNK = 64
N_BRANCHES = 3
EPS = 1e-6

LANES = 128
BF16_SUBLANES = 16
VMEM_LIMIT = 52 * 1024 * 1024

BLK = 512
COL_GATE = 0
COL_CQKV = 6
COL_CZ = 9
COL_U = 10
COL_SV = 11
COL_BQ = 12
COL_BK = 13
COL_BV = 14
N_MAIN = 15 * BLK
GATE_TAB = 128


def _cparams(sem):
    return pltpu.CompilerParams(dimension_semantics=sem, vmem_limit_bytes=VMEM_LIMIT)


def _mm(a, b):
    return jnp.dot(a.astype(BF16), b.astype(BF16), preferred_element_type=F32)


def _mm_nt(a, b):
    return lax.dot_general(a.astype(BF16), b.astype(BF16), (((1,), (1,)), ((), ())),
                           preferred_element_type=F32)


def _mm_tn(a, b):
    return lax.dot_general(a.astype(BF16), b.astype(BF16), (((0,), (0,)), ((), ())),
                           preferred_element_type=F32)


def _silu(x):
    return x * jax.nn.sigmoid(x)


def _mod_kernel(c_ref, w_ref, b_ref, o_ref):
    s = _silu(c_ref[...])
    o_ref[...] = jnp.dot(s, w_ref[...], preferred_element_type=F32, precision=HIGHEST) + b_ref[...]


def _modulation(cs, w, b):
    r, d = cs.shape
    n = w.shape[1]
    tn = 1536
    return pl.pallas_call(
        _mod_kernel,
        grid=(n // tn,),
        in_specs=[pl.BlockSpec((r, d), lambda j: (0, 0)),
                  pl.BlockSpec((d, tn), lambda j: (0, j)),
                  pl.BlockSpec((1, tn), lambda j: (0, j))],
        out_specs=pl.BlockSpec((r, tn), lambda j: (0, j)),
        out_shape=jax.ShapeDtypeStruct((r, n), F32),
        compiler_params=_cparams(("arbitrary",)),
        name="modulation",
    )(cs, w, b)


def _nmm_kernel(x_ref, g_ref, sh_ref, sc_ref, w_ref, o_ref, h_ref):
    @pl.when(pl.program_id(2) == 0)
    def _():
        x = x_ref[0]
        y = x * lax.rsqrt(jnp.mean(x * x, axis=-1, keepdims=True) + EPS) * g_ref[...]
        h_ref[...] = (y * (1.0 + sc_ref[0]) + sh_ref[0]).astype(BF16)

    o_ref[0] = jnp.dot(h_ref[...], w_ref[...], preferred_element_type=F32).astype(o_ref.dtype)


def _nmm2_kernel(x_ref, g_ref, sh_ref, sc_ref, w_ref, w2_ref, o_ref, o2_ref, h_ref):
    @pl.when(pl.program_id(2) == 0)
    def _():
        x = x_ref[0]
        y = x * lax.rsqrt(jnp.mean(x * x, axis=-1, keepdims=True) + EPS) * g_ref[...]
        h_ref[...] = (y * (1.0 + sc_ref[0]) + sh_ref[0]).astype(BF16)
        o2_ref[0] = jnp.dot(h_ref[...], w2_ref[...], preferred_element_type=F32)

    o_ref[0] = jnp.dot(h_ref[...], w_ref[...], preferred_element_type=F32).astype(o_ref.dtype)


def _norm_mod_matmul2(x, g, shift, scale, w, w2, tm, tn, name):
    bsz, n, d = x.shape
    nc, nc2 = w.shape[1], w2.shape[1]
    tm = min(tm, n)
    return pl.pallas_call(
        _nmm2_kernel,
        grid=(bsz, n // tm, nc // tn),
        in_specs=[pl.BlockSpec((1, tm, d), lambda b, i, j: (b, i, 0)),
                  pl.BlockSpec((1, d), lambda b, i, j: (0, 0)),
                  pl.BlockSpec((1, 1, d), lambda b, i, j: (b, 0, 0)),
                  pl.BlockSpec((1, 1, d), lambda b, i, j: (b, 0, 0)),
                  pl.BlockSpec((d, tn), lambda b, i, j: (0, j)),
                  pl.BlockSpec((d, nc2), lambda b, i, j: (0, 0))],
        out_specs=[pl.BlockSpec((1, tm, tn), lambda b, i, j: (b, i, j)),
                   pl.BlockSpec((1, tm, nc2), lambda b, i, j: (b, i, 0))],
        out_shape=[jax.ShapeDtypeStruct((bsz, n, nc), BF16),
                   jax.ShapeDtypeStruct((bsz, n, nc2), F32)],
        scratch_shapes=[pltpu.VMEM((tm, d), BF16)],
        compiler_params=_cparams(("parallel", "parallel", "arbitrary")),
        name=name,
    )(x, g, shift, scale, w, w2)


def _norm_mod_matmul(x, g, shift, scale, w, out_dtype, tm, tn, name):
    bsz, n, d = x.shape
    nc = w.shape[1]
    tm = min(tm, n)
    return pl.pallas_call(
        _nmm_kernel,
        grid=(bsz, n // tm, nc // tn),
        in_specs=[pl.BlockSpec((1, tm, d), lambda b, i, j: (b, i, 0)),
                  pl.BlockSpec((1, d), lambda b, i, j: (0, 0)),
                  pl.BlockSpec((1, 1, d), lambda b, i, j: (b, 0, 0)),
                  pl.BlockSpec((1, 1, d), lambda b, i, j: (b, 0, 0)),
                  pl.BlockSpec((d, tn), lambda b, i, j: (0, j))],
        out_specs=pl.BlockSpec((1, tm, tn), lambda b, i, j: (b, i, j)),
        out_shape=jax.ShapeDtypeStruct((bsz, n, nc), out_dtype),
        scratch_shapes=[pltpu.VMEM((tm, d), BF16)],
        compiler_params=_cparams(("parallel", "parallel", "arbitrary")),
        name=name,
    )(x, g, shift, scale, w)


def _sgu_kernel(u_ref, v_ref, ng_ref, w_ref, b_ref, o_ref):
    tm = u_ref.shape[1]
    for c in range(tm // A_CHUNK):
        rows = slice(c * A_CHUNK, (c + 1) * A_CHUNK)
        for g in range(A_GROUPS):
            cols = slice(g * A_GROUP_DIM, (g + 1) * A_GROUP_DIM)
            u = jax.nn.gelu(u_ref[0, rows, cols].astype(F32))
            v = jax.nn.gelu(v_ref[0, rows, cols].astype(F32))
            vn = v * lax.rsqrt(jnp.mean(v * v, axis=-1, keepdims=True) + EPS) * ng_ref[g:g + 1, :]
            mixed = _mm(w_ref[g], vn) + b_ref[:, g:g + 1]
            o_ref[0, rows, cols] = (u * mixed).astype(o_ref.dtype)


def _spatial_gating(p, ng, w, b_t, tm):
    bsz, n, _ = p.shape
    tm = min(tm, n)
    return pl.pallas_call(
        _sgu_kernel,
        grid=(bsz, n // tm),
        in_specs=[pl.BlockSpec((1, tm, BLK), lambda b, i: (b, i, COL_U)),
                  pl.BlockSpec((1, tm, BLK), lambda b, i: (b, i, COL_SV)),
                  pl.BlockSpec(ng.shape, lambda b, i: (0, 0)),
                  pl.BlockSpec(w.shape, lambda b, i: (0, 0, 0)),
                  pl.BlockSpec(b_t.shape, lambda b, i: (0, 0))],
        out_specs=pl.BlockSpec((1, tm, A_WIDTH), lambda b, i: (b, i, 0)),
        out_shape=jax.ShapeDtypeStruct((bsz, n, A_WIDTH), BF16),
        compiler_params=_cparams(("parallel", "parallel")),
        name="spatial_gating",
    )(p, p, ng, w, b_t)


def _seg_rms(x, g, m_ref):
    x2 = x * x
    hi = x2.astype(BF16)
    lo = (x2 - hi.astype(F32)).astype(BF16)
    ms = (jnp.dot(hi, m_ref[...], preferred_element_type=F32)
          + jnp.dot(lo, m_ref[...], preferred_element_type=F32))
    return x * lax.rsqrt(ms + EPS) * g


def _rope(x, cos, sin_signed):
    half = B_HEAD_DIM // 4
    width = x.shape[1]
    lane = lax.broadcasted_iota(jnp.int32, x.shape, 1)
    first = (lane & (2 * half - 1)) < half
    partner = jnp.where(first, pltpu.roll(x, width - half, axis=1), pltpu.roll(x, half, axis=1))
    return x * cos + partner * sin_signed


def _transpose_store(x, eye_ref, o_ref):
    xt = lax.dot_general(eye_ref[...], x.astype(BF16), (((1,), (1,)), ((), ())), preferred_element_type=F32)
    o_ref[0] = xt.astype(o_ref.dtype)


Q_SCALE = B_HEAD_DIM ** -0.5 * math.log2(math.e)
ATTN_KEY_BLOCK = 512


def _qk_prep_rope_kernel(q_ref, k_ref, qg_ref, kg_ref, m_ref, eye_ref, cos_ref, sin_ref, ko_ref, qt_ref):
    cos = cos_ref[...]
    sin = sin_ref[...]
    q = _rope(_seg_rms(q_ref[0].astype(F32), qg_ref[...], m_ref), cos, sin)
    k = _rope(_seg_rms(k_ref[0].astype(F32), kg_ref[...], m_ref), cos, sin)
    ko_ref[0] = k.astype(ko_ref.dtype)
    _transpose_store(q * Q_SCALE, eye_ref, qt_ref)


def _qk_prep_plain_kernel(q_ref, k_ref, qg_ref, kg_ref, m_ref, eye_ref, ko_ref, qt_ref):
    q = _seg_rms(q_ref[0].astype(F32), qg_ref[...], m_ref)
    k = _seg_rms(k_ref[0].astype(F32), kg_ref[...], m_ref)
    ko_ref[0] = k.astype(ko_ref.dtype)
    _transpose_store(q * Q_SCALE, eye_ref, qt_ref)


def _qk_prep(p, qg, kg, m_avg, eye, tables, tm):
    bsz, n, _ = p.shape
    tm = min(tm, n)
    in_specs = [pl.BlockSpec((1, tm, BLK), lambda b, i: (b, i, COL_BQ)),
                pl.BlockSpec((1, tm, BLK), lambda b, i: (b, i, COL_BK)),
                pl.BlockSpec((1, B_QK), lambda b, i: (0, 0)),
                pl.BlockSpec((1, B_QK), lambda b, i: (0, 0)),
                pl.BlockSpec((B_QK, B_QK), lambda b, i: (0, 0)),
                pl.BlockSpec((B_QK, B_QK), lambda b, i: (0, 0))]
    args = [p, p, qg, kg, m_avg, eye]
    if tables is None:
        body = _qk_prep_plain_kernel
    else:
        body = _qk_prep_rope_kernel
        in_specs += [pl.BlockSpec((tm, B_QK), lambda b, i: (i, 0)),
                     pl.BlockSpec((tm, B_QK), lambda b, i: (i, 0))]
        args += list(tables)
    return pl.pallas_call(
        body,
        grid=(bsz, n // tm),
        in_specs=in_specs,
        out_specs=[pl.BlockSpec((1, tm, B_QK), lambda b, i: (b, i, 0)),
                   pl.BlockSpec((1, B_QK, tm), lambda b, i: (b, 0, i))],
        out_shape=[jax.ShapeDtypeStruct((bsz, n, B_QK), BF16),
                   jax.ShapeDtypeStruct((bsz, B_QK, n), BF16)],
        compiler_params=_cparams(("parallel", "parallel")),
        name="qk_prep",
    )(*args)


def _attn_kernel(*refs, has_lat, lam_init):
    if has_lat:
        lam_ref, qt_ref, kc_ref, vc_ref, kx_ref, vx_ref, g_ref, eye_ref, o_ref, vtc_s, vtx_s = refs
    else:
        lam_ref, qt_ref, kc_ref, vc_ref, g_ref, eye_ref, o_ref, vtc_s = refs
    hv = vc_ref.shape[2]

    @pl.when(pl.program_id(2) == 0)
    def _():
        pairs = ((vc_ref, vtc_s), (vx_ref, vtx_s)) if has_lat else ((vc_ref, vtc_s),)
        for v_ref, vt_s in pairs:
            vt = lax.dot_general(eye_ref[...], v_ref[0], (((1,), (1,)), ((), ())), preferred_element_type=F32)
            vt_s[:hv, :] = vt.astype(BF16)
            vt_s[hv:, :] = jnp.ones((BF16_SUBLANES, vt_s.shape[1]), BF16)

    qt = qt_ref[0]
    row = lax.broadcasted_iota(jnp.int32, qt.shape, 0)
    zero = jnp.zeros_like(qt)
    kblk = min(ATTN_KEY_BLOCK, kx_ref.shape[1]) if has_lat else 0
    assert not has_lat or kx_ref.shape[1] % kblk == 0
    n_blk = kx_ref.shape[1] // kblk if has_lat else 0
    blk = lambda j: slice((j - 1) * kblk, j * kblk)
    keys = lambda j: kc_ref[0] if j == 0 else kx_ref[0, blk(j), :]
    vals_t = lambda j: vtc_s[...] if j == 0 else vtx_s[:, blk(j)]
    q_sub = [jnp.where(row < B_HEAD_DIM, qt, zero), jnp.where(row >= B_HEAD_DIM, qt, zero)]
    score = lambda m, j: jnp.dot(keys(j), q_sub[m], preferred_element_type=F32)
    pv = lambda j, p: jnp.dot(vals_t(j), p, preferred_element_type=F32)

    m_run, acc = [None, None], [None, None]
    s_next = [score(m, 0) for m in range(2)]
    for j in range(n_blk + 1):
        s_cur = s_next
        if j < n_blk:
            s_next = [score(m, j + 1) for m in range(2)]
        for m in range(2):
            s = s_cur[m]
            blk_max = jnp.max(s, axis=0, keepdims=True)
            if j == 0:
                m_new = blk_max
                acc[m] = pv(j, jnp.exp2(s - m_new).astype(BF16))
            else:
                m_new = jnp.maximum(m_run[m], blk_max)
                alpha = jnp.exp2(m_run[m] - m_new)
                acc[m] = alpha * acc[m] + pv(j, jnp.exp2(s - m_new).astype(BF16))
            m_run[m] = m_new
    outs = [a[:hv] / a[hv:hv + 1] for a in acc]
    lv = lam_ref[...]
    lam = (jnp.exp(jnp.sum(lv[0:1] * lv[1:2], axis=-1, keepdims=True))
           - jnp.exp(jnp.sum(lv[2:3] * lv[3:4], axis=-1, keepdims=True)) + lam_init)
    o = outs[0] - lam * outs[1]
    o = o * lax.rsqrt(jnp.mean(o * o, axis=0, keepdims=True) + EPS) * g_ref[...]
    o_ref[0] = (o * (1.0 - lam_init)).T.astype(o_ref.dtype)


def _diff_attention(lam_vecs, qt, kc, pc, kx, px, subln_g, eye, lam_init, tq):
    bsz, _, n = qt.shape
    nctx = kc.shape[1]
    tq = min(tq, n)
    hv = B_V // B_HEADS
    vcol = COL_BV * (BLK // hv)
    has_lat = kx is not None
    in_specs = [pl.BlockSpec(lam_vecs.shape, lambda b, h, i: (0, 0)),
                pl.BlockSpec((1, hv, tq), lambda b, h, i: (b, h, i)),
                pl.BlockSpec((1, nctx, hv), lambda b, h, i: (b, 0, h)),
                pl.BlockSpec((1, nctx, hv), lambda b, h, i: (b, 0, vcol + h))]
    args = [lam_vecs, qt, kc, pc]
    scratch = [pltpu.VMEM((hv + BF16_SUBLANES, nctx), BF16)]
    if has_lat:
        nlat = kx.shape[1]
        in_specs += [pl.BlockSpec((1, nlat, hv), lambda b, h, i: (b, 0, h)),
                     pl.BlockSpec((1, nlat, hv), lambda b, h, i: (b, 0, vcol + h))]
        args += [kx, px]
        scratch.append(pltpu.VMEM((hv + BF16_SUBLANES, nlat), BF16))
    in_specs += [pl.BlockSpec((hv, 1), lambda b, h, i: (0, 0)),
                 pl.BlockSpec((hv, hv), lambda b, h, i: (0, 0))]
    args += [subln_g, eye]
    return pl.pallas_call(
        functools.partial(_attn_kernel, has_lat=has_lat, lam_init=lam_init),
        grid=(bsz, B_HEADS, n // tq),
        in_specs=in_specs,
        out_specs=pl.BlockSpec((1, tq, hv), lambda b, h, i: (b, i, h)),
        out_shape=jax.ShapeDtypeStruct((bsz, n, B_V), BF16),
        scratch_shapes=scratch,
        compiler_params=_cparams(("parallel", "parallel", "arbitrary")),
        name="diff_attention",
    )(*args)


def _conv3(cur, prev_row, next_row, w):
    t = cur.shape[0]
    rid = lax.broadcasted_iota(jnp.int32, cur.shape, 0)
    xm = jnp.where(rid == 0, prev_row, pltpu.roll(cur, 1, axis=0))
    xp = jnp.where(rid == t - 1, next_row, pltpu.roll(cur, t - 1, axis=0))
    return w[0:1] * xm + w[1:2] * cur + w[2:3] * xp


def _halo_rows(prev_ref, next_ref, i, n_tiles):
    h = prev_ref.shape[1]
    prev_row = prev_ref[0, h - 1:h, :].astype(F32)
    next_row = next_ref[0, 0:1, :].astype(F32)
    prev_row = jnp.where(i > 0, prev_row, jnp.zeros_like(prev_row))
    next_row = jnp.where(i < n_tiles - 1, next_row, jnp.zeros_like(next_row))
    return prev_row, next_row


def _halo_specs(tm, width, col, n):
    r = tm // BF16_SUBLANES
    last = n // BF16_SUBLANES - 1
    cur = pl.BlockSpec((1, tm, width), lambda b, i, *_: (b, i, col(*_)))
    prev = pl.BlockSpec((1, BF16_SUBLANES, width),
                        lambda b, i, *_: (b, jnp.maximum(i * r - 1, 0), col(*_)))
    nxt = pl.BlockSpec((1, BF16_SUBLANES, width),
                       lambda b, i, *_: (b, jnp.minimum((i + 1) * r, last), col(*_)))
    return cur, prev, nxt


def _gdn_prep_kernel(x_ref, xp_ref, xn_ref, cw_ref, lg_ref, alog_ref, dtb_ref, eye_ref,
                     q_ref, k_ref, kt_ref, v_ref, tab_ref, row_ref):
    i = pl.program_id(1)
    tm = x_ref.shape[1]
    prev_row, next_row = _halo_rows(xp_ref, xn_ref, i, pl.num_programs(1))
    y = _silu(_conv3(x_ref[0].astype(F32), prev_row, next_row, cw_ref[...]))
    for h in range(C_HEADS):
        cq = slice(h * C_HEAD_DIM, (h + 1) * C_HEAD_DIM)
        ck = slice(C_WIDTH + h * C_HEAD_DIM, C_WIDTH + (h + 1) * C_HEAD_DIM)
        q = y[:, cq]
        k = y[:, ck]
        qn = q * lax.rsqrt(jnp.sum(q * q, axis=-1, keepdims=True) + EPS)
        kn = k * lax.rsqrt(jnp.sum(k * k, axis=-1, keepdims=True) + EPS)
        q_ref[0, :, cq] = (qn * C_HEAD_DIM ** -0.5).astype(q_ref.dtype)
        k_ref[0, :, cq] = kn.astype(k_ref.dtype)
    _transpose_store(k_ref[0], eye_ref, kt_ref)
    v_ref[0] = y[:, 2 * C_WIDTH:].astype(v_ref.dtype)

    lg = lg_ref[0]
    col = lax.broadcasted_iota(jnp.int32, lg.shape, 1)
    beta = jax.nn.sigmoid(lg)
    z = lg + dtb_ref[...]
    softplus = jnp.maximum(z, 0.0) + jnp.log(1.0 + jnp.exp(-jnp.abs(z)))
    g = -jnp.exp(alog_ref[...]) * softplus
    g = jnp.where((col >= 2 * C_HEADS) & (col < 4 * C_HEADS), g, 0.0)
    ri = lax.broadcasted_iota(jnp.int32, (tm, tm), 0)
    ci = lax.broadcasted_iota(jnp.int32, (tm, tm), 1)
    same = lax.shift_right_logical(ri, 6) == lax.shift_right_logical(ci, 6)
    lower = jnp.where(same & (ci <= ri), 1.0, 0.0).astype(F32)
    upper = jnp.where(same & (ci >= ri), 1.0, 0.0).astype(F32)
    cum_f = jnp.dot(lower, g, preferred_element_type=F32, precision=HIGHEST)
    cum_b = jnp.dot(upper, g, preferred_element_type=F32, precision=HIGHEST)
    cum = jnp.where(col >= 3 * C_HEADS, cum_b, cum_f)
    tab = jnp.where(col < 2 * C_HEADS, beta, cum)
    tab_ref[0] = tab
    sel = (lax.broadcasted_iota(jnp.int32, (4 * C_HEADS, GATE_TAB), 0)
           == lax.broadcasted_iota(jnp.int32, (4 * C_HEADS, GATE_TAB), 1)).astype(F32)
    row_ref[0] = lax.dot_general(sel, tab, (((1,), (1,)), ((), ())), preferred_element_type=F32,
                                 precision=HIGHEST)


def _gdn_prep(p, logits, conv_w, alog_row, dtb_row, eye, tm):
    bsz, n, _ = p.shape
    tm = min(tm, n)
    cur, prev, nxt = _halo_specs(tm, 3 * C_WIDTH, lambda: COL_CQKV // 3, n)
    bf = jax.ShapeDtypeStruct((bsz, n, C_WIDTH), BF16)
    tok = pl.BlockSpec((1, tm, C_WIDTH), lambda b, i: (b, i, 0))
    return pl.pallas_call(
        _gdn_prep_kernel,
        grid=(bsz, n // tm),
        in_specs=[cur, prev, nxt,
                  pl.BlockSpec(conv_w.shape, lambda b, i: (0, 0)),
                  pl.BlockSpec((1, tm, GATE_TAB), lambda b, i: (b, i, 0)),
                  pl.BlockSpec((1, GATE_TAB), lambda b, i: (0, 0)),
                  pl.BlockSpec((1, GATE_TAB), lambda b, i: (0, 0)),
                  pl.BlockSpec(eye.shape, lambda b, i: (0, 0))],
        out_specs=[tok, tok,
                   pl.BlockSpec((1, C_WIDTH, tm), lambda b, i: (b, 0, i)),
                   tok,
                   pl.BlockSpec((1, tm, GATE_TAB), lambda b, i: (b, i, 0)),
                   pl.BlockSpec((1, 4 * C_HEADS, tm), lambda b, i: (b, 0, i))],
        out_shape=[bf, bf, jax.ShapeDtypeStruct((bsz, C_WIDTH, n), BF16), bf,
                   jax.ShapeDtypeStruct((bsz, n, GATE_TAB), F32),
                   jax.ShapeDtypeStruct((bsz, 4 * C_HEADS, n), F32)],
        compiler_params=_cparams(("parallel", "parallel")),
        name="gdn_prep",
    )(p, p, p, conv_w, logits, alog_row, dtb_row, eye)


def _chunk_masks(d):
    ri = lax.broadcasted_iota(jnp.int32, (GDN_CHUNK, GDN_CHUNK), 0)
    ci = lax.broadcasted_iota(jnp.int32, (GDN_CHUNK, GDN_CHUNK), 1)
    if d == 0:
        return ri >= ci, ri > ci
    return ri <= ci, ri < ci


def _gate_views(tab, row, rows, d, h):
    beta = tab[rows, d * C_HEADS + h:d * C_HEADS + h + 1]
    gcol = tab[rows, 2 * C_HEADS + d * C_HEADS + h:2 * C_HEADS + d * C_HEADS + h + 1]
    grow = row[2 * C_HEADS + d * C_HEADS + h:2 * C_HEADS + d * C_HEADS + h + 1, rows]
    return beta, gcol, grow


def _gdn_local_kernel(k_ref, kt_ref, v_ref, tab_ref, row_ref, uf_ref, wf_ref, ub_ref, wb_ref):
    tl = k_ref.shape[1]
    tab = tab_ref[0]
    row = row_ref[0]
    nh = C_HEADS
    cw = nh * GDN_CHUNK
    r4 = lax.broadcasted_iota(jnp.int32, (cw, cw), 0)
    c4 = lax.broadcasted_iota(jnp.int32, (cw, cw), 1)
    bd_mask = lax.shift_right_logical(r4, 6) == lax.shift_right_logical(c4, 6)
    r64 = lax.broadcasted_iota(jnp.int32, (GDN_CHUNK, cw), 0)
    c64 = lax.broadcasted_iota(jnp.int32, (GDN_CHUNK, cw), 1) & (GDN_CHUNK - 1)
    eye_cat = jnp.where(r64 == c64, 1.0, 0.0)
    bd_zero = jnp.zeros((cw, cw), BF16)

    def block_diag(b):
        return jnp.where(bd_mask, jnp.concatenate([b.astype(BF16)] * nh, axis=0), bd_zero)

    def off_block(lg):
        shr = lax.shift_right_logical
        return (shr(r64, lg) != shr(c64, lg)) & (shr(r64, lg + 1) == shr(c64, lg + 1))

    out_refs = ((uf_ref, wf_ref), (ub_ref, wb_ref))
    units = [(c, d) for c in range(tl // GDN_CHUNK) for d in range(2)]
    a_cat, rhs = [], []
    for c, d in units:
        rows = slice(c * GDN_CHUNK, (c + 1) * GDN_CHUNK)
        _, strict = _chunk_masks(d)
        a_list, rhs_list = [], []
        for h in range(nh):
            cols = slice(h * C_HEAD_DIM, (h + 1) * C_HEAD_DIM)
            k = k_ref[0, rows, cols].astype(F32)
            v = v_ref[0, rows, cols].astype(F32)
            beta, gcol, grow = _gate_views(tab, row, rows, d, h)
            seg = jnp.exp(jnp.minimum(gcol - grow, 0.0))
            kb = k * beta
            a_list.append(jnp.where(strict, _mm(kb, kt_ref[0, cols, rows]) * seg, 0.0))
            rhs_list.append(jnp.concatenate([v * beta, kb * jnp.exp(gcol)], axis=1).astype(BF16))
        a_cat.append(jnp.concatenate(a_list, axis=1))
        rhs.append(rhs_list)
    t = [eye_cat - jnp.where(off_block(0), a, 0.0) for a in a_cat]
    for lg in range(1, 6):
        x = [_mm(jnp.where(off_block(lg), a, 0.0), block_diag(tu)) for a, tu in zip(a_cat, t)]
        t = [tu - _mm(tu, block_diag(xu)) for tu, xu in zip(t, x)]
    for (c, d), tu, rhs_list in zip(units, t, rhs):
        rows = slice(c * GDN_CHUNK, (c + 1) * GDN_CHUNK)
        u_ref, w_ref = out_refs[d]
        for h in range(nh):
            cols = slice(h * C_HEAD_DIM, (h + 1) * C_HEAD_DIM)
            uw = _mm(tu[:, h * GDN_CHUNK:(h + 1) * GDN_CHUNK], rhs_list[h])
            u_ref[0, rows, cols] = uw[:, :C_HEAD_DIM].astype(u_ref.dtype)
            w_ref[0, rows, cols] = uw[:, C_HEAD_DIM:].astype(w_ref.dtype)


def _gdn_local(k, kt, v, tab, row, tl):
    bsz, n, _ = k.shape
    tl = min(tl, n)
    tok = pl.BlockSpec((1, tl, C_WIDTH), lambda b, i: (b, i, 0))
    out = jax.ShapeDtypeStruct((bsz, n, C_WIDTH), BF16)
    return pl.pallas_call(
        _gdn_local_kernel,
        grid=(bsz, n // tl),
        in_specs=[tok,
                  pl.BlockSpec((1, C_WIDTH, tl), lambda b, i: (b, 0, i)),
                  tok,
                  pl.BlockSpec((1, tl, GATE_TAB), lambda b, i: (b, i, 0)),
                  pl.BlockSpec((1, 4 * C_HEADS, tl), lambda b, i: (b, 0, i))],
        out_specs=[tok, tok, tok, tok],
        out_shape=[out, out, out, out],
        compiler_params=_cparams(("parallel", "parallel")),
        name="gdn_local",
    )(k, kt, v, tab, row)


SCAN_TILE = 2 * GDN_CHUNK


def _gdn_scan_kernel(qf_ref, ktf_ref, uf_ref, wf_ref, tabf_ref, rowf_ref,
                     qb_ref, ktb_ref, ub_ref, wb_ref, tabb_ref, rowb_ref, s0_ref,
                     of_ref, ob_ref, sfin_ref, s_ref):
    step = pl.program_id(1)

    @pl.when(step == 0)
    def _():
        s_ref[...] = s0_ref[0]

    dirs = ((qf_ref, ktf_ref, uf_ref, wf_ref, tabf_ref, rowf_ref, of_ref),
            (qb_ref, ktb_ref, ub_ref, wb_ref, tabb_ref, rowb_ref, ob_ref))
    n_chunks = SCAN_TILE // GDN_CHUNK
    chains = [(d, h) for d in range(2) for h in range(C_HEADS)]
    tabs = [r[4][0] for r in dirs]
    rowtabs = [r[5][0] for r in dirs]
    for sub in range(n_chunks):
        local = []
        for d, h in chains:
            q_ref, kt_ref, u_ref, w_ref, _, _, _ = dirs[d]
            c = sub if d == 0 else n_chunks - 1 - sub
            rows = slice(c * GDN_CHUNK, (c + 1) * GDN_CHUNK)
            cols = slice(h * C_HEAD_DIM, (h + 1) * C_HEAD_DIM)
            incl, _ = _chunk_masks(d)
            q = q_ref[0, rows, cols].astype(F32)
            kt = kt_ref[0, cols, rows].astype(F32)
            _, gcol, grow = _gate_views(tabs[d], rowtabs[d], rows, d, h)
            glast = gcol[GDN_CHUNK - 1:GDN_CHUNK] if d == 0 else gcol[0:1]
            seg = jnp.where(incl, jnp.exp(jnp.minimum(gcol - grow, 0.0)), 0.0)
            intra = (_mm(q, kt) * seg).astype(BF16)
            qd = (q * jnp.exp(gcol)).astype(BF16)
            kt_tail = (kt * jnp.exp(glast - grow)).astype(BF16)
            lhs = jnp.concatenate([w_ref[0, rows, cols], qd], axis=0)
            local.append((rows, cols, lhs, intra, kt_tail, jnp.exp(glast)))
        states = [s_ref[d, h] for d, h in chains]
        ws_qs = [_mm(loc[2], s) for loc, s in zip(local, states)]
        for (d, h), loc, s, wq in zip(chains, local, states, ws_qs):
            rows, cols, _, intra, kt_tail, gtot = loc
            u_ref, o_ref = dirs[d][2], dirs[d][6]
            v_new = (u_ref[0, rows, cols].astype(F32) - wq[:GDN_CHUNK]).astype(BF16)
            o_ref[0, rows, cols] = wq[GDN_CHUNK:] + jnp.dot(intra, v_new, preferred_element_type=F32)
            s_ref[d, h] = s * gtot + jnp.dot(kt_tail, v_new, preferred_element_type=F32)

    @pl.when(step == pl.num_programs(1) - 1)
    def _():
        sfin_ref[0] = s_ref[...]


def _gdn_scan(q, kt, uf, wf, ub, wb, tab, row, s0):
    bsz, n, _ = q.shape
    nt = n // SCAN_TILE
    tok_f = pl.BlockSpec((1, SCAN_TILE, C_WIDTH), lambda b, i: (b, i, 0))
    tok_b = pl.BlockSpec((1, SCAN_TILE, C_WIDTH), lambda b, i: (b, nt - 1 - i, 0))
    ktr_f = pl.BlockSpec((1, C_WIDTH, SCAN_TILE), lambda b, i: (b, 0, i))
    ktr_b = pl.BlockSpec((1, C_WIDTH, SCAN_TILE), lambda b, i: (b, 0, nt - 1 - i))
    tab_f = pl.BlockSpec((1, SCAN_TILE, GATE_TAB), lambda b, i: (b, i, 0))
    tab_b = pl.BlockSpec((1, SCAN_TILE, GATE_TAB), lambda b, i: (b, nt - 1 - i, 0))
    row_f = pl.BlockSpec((1, 4 * C_HEADS, SCAN_TILE), lambda b, i: (b, 0, i))
    row_b = pl.BlockSpec((1, 4 * C_HEADS, SCAN_TILE), lambda b, i: (b, 0, nt - 1 - i))
    st = pl.BlockSpec((1, 2, C_HEADS, C_HEAD_DIM, C_HEAD_DIM), lambda b, i: (b, 0, 0, 0, 0))
    o_sds = jax.ShapeDtypeStruct((bsz, n, C_WIDTH), F32)
    return pl.pallas_call(
        _gdn_scan_kernel,
        grid=(bsz, nt),
        in_specs=[tok_f, ktr_f, tok_f, tok_f, tab_f, row_f,
                  tok_b, ktr_b, tok_b, tok_b, tab_b, row_b, st],
        out_specs=[tok_f, tok_b, st],
        out_shape=[o_sds, o_sds, jax.ShapeDtypeStruct(s0.shape, F32)],
        scratch_shapes=[pltpu.VMEM((2, C_HEADS, C_HEAD_DIM, C_HEAD_DIM), F32)],
        compiler_params=_cparams(("parallel", "arbitrary")),
        name="gdn_scan",
    )(q, kt, uf, wf, tab, row, q, kt, ub, wb, tab, row, s0)


def _merge_kernel(x_ref, g0_ref, g1_ref, g2_ref, ya_ref, yb_ref, of_ref, ob_ref, z_ref, gn_ref,
                  gt_ref, wa_ref, wb_ref, wc_ref, wo_ref, o_ref):
    o = of_ref[0] + ob_ref[0]
    z = z_ref[0].astype(F32)
    parts = []
    for h in range(C_HEADS):
        cols = slice(h * C_HEAD_DIM, (h + 1) * C_HEAD_DIM)
        oh = o[:, cols]
        parts.append(oh * lax.rsqrt(jnp.mean(oh * oh, axis=-1, keepdims=True) + EPS))
    yc = jnp.concatenate(parts, axis=1) * gn_ref[...] * _silu(z)
    y = (jax.nn.sigmoid(g0_ref[0].astype(F32)) * jnp.dot(ya_ref[0], wa_ref[...], preferred_element_type=F32)
         + jax.nn.sigmoid(g1_ref[0].astype(F32)) * jnp.dot(yb_ref[0], wb_ref[...], preferred_element_type=F32)
         + jax.nn.sigmoid(g2_ref[0].astype(F32)) * _mm(yc, wc_ref[...]))
    o_ref[0] = x_ref[0] + gt_ref[0] * _mm(y, wo_ref[...])


def _merge(x, p, ya, yb, of, ob, gn, gt, wa, wb, wc, wo, tm):
    bsz, n, d = x.shape
    tm = min(tm, n)
    tok = lambda width, col: pl.BlockSpec((1, tm, width), lambda b, i: (b, i, col))
    full = lambda a: pl.BlockSpec(a.shape, lambda b, i: (0,) * a.ndim)
    return pl.pallas_call(
        _merge_kernel,
        grid=(bsz, n // tm),
        in_specs=[tok(d, 0), tok(d, 0), tok(d, 1), tok(d, 2),
                  tok(A_WIDTH, 0), tok(B_V, 0), tok(C_WIDTH, 0), tok(C_WIDTH, 0), tok(BLK, COL_CZ),
                  full(gn), pl.BlockSpec((1, 1, d), lambda b, i: (b, 0, 0)),
                  full(wa), full(wb), full(wc), full(wo)],
        out_specs=tok(d, 0),
        out_shape=jax.ShapeDtypeStruct(x.shape, F32),
        compiler_params=_cparams(("parallel", "parallel")),
        name="merge",
    )(x, p, p, p, ya, yb, of, ob, p, gn, gt, wa, wb, wc, wo)


def _ffn_down_kernel(x_ref, ug_ref, ugp_ref, ugn_ref, uv_ref, uvp_ref, uvn_ref, cwg_ref, cwv_ref,
                     wd_ref, gt_ref, o_ref, acc_ref):
    i = pl.program_id(1)
    kk = pl.program_id(2)
    nt = pl.num_programs(1)

    @pl.when(kk == 0)
    def _():
        acc_ref[...] = jnp.zeros(acc_ref.shape, F32)

    gp, gn = _halo_rows(ugp_ref, ugn_ref, i, nt)
    vp, vn = _halo_rows(uvp_ref, uvn_ref, i, nt)
    gate = _conv3(ug_ref[0].astype(F32), gp, gn, cwg_ref[...])
    val = _conv3(uv_ref[0].astype(F32), vp, vn, cwv_ref[...])
    acc_ref[...] += _mm(_silu(gate) * val, wd_ref[...])

    @pl.when(kk == pl.num_programs(2) - 1)
    def _():
        o_ref[0] = x_ref[0] + gt_ref[0] * acc_ref[...]


def _ffn_down(x, u, conv_w, wd, gt, tm, tk):
    bsz, n, d = x.shape
    f = wd.shape[0]
    tm = min(tm, n)
    nk = f // tk
    g_cur, g_prev, g_next = _halo_specs(tm, tk, lambda k: k, n)
    v_cur, v_prev, v_next = _halo_specs(tm, tk, lambda k: nk + k, n)
    return pl.pallas_call(
        _ffn_down_kernel,
        grid=(bsz, n // tm, nk),
        in_specs=[pl.BlockSpec((1, tm, d), lambda b, i, k: (b, i, 0)),
                  g_cur, g_prev, g_next, v_cur, v_prev, v_next,
                  pl.BlockSpec((3, tk), lambda b, i, k: (0, k)),
                  pl.BlockSpec((3, tk), lambda b, i, k: (0, nk + k)),
                  pl.BlockSpec((tk, d), lambda b, i, k: (k, 0)),
                  pl.BlockSpec((1, 1, d), lambda b, i, k: (b, 0, 0))],
        out_specs=pl.BlockSpec((1, tm, d), lambda b, i, k: (b, i, 0)),
        out_shape=jax.ShapeDtypeStruct(x.shape, F32),
        scratch_shapes=[pltpu.VMEM((tm, d), F32)],
        compiler_params=_cparams(("parallel", "parallel", "arbitrary")),
        name="ffn_down",
    )(x, u, u, u, u, u, u, conv_w, conv_w, wd, gt)


def _rope_tables(n):
    n_freq = B_HEAD_DIM // 4
    inv_freq = ROPE_BASE ** (-jnp.arange(n_freq, dtype=F32) / n_freq)
    pos = jnp.arange(n, dtype=jnp.int32)
    row = (pos // GRID_W).astype(F32)
    col = (pos % GRID_W).astype(F32)
    ang_r = row[:, None] * inv_freq
    ang_c = col[:, None] * inv_freq
    ang = jnp.concatenate([ang_r, ang_r, ang_c, ang_c], axis=-1)
    sign = jnp.tile(jnp.concatenate([-jnp.ones(n_freq, F32), jnp.ones(n_freq, F32)]), 2)
    reps = B_QK // B_HEAD_DIM
    return jnp.tile(jnp.cos(ang), (1, reps)), jnp.tile(jnp.sin(ang) * sign, (1, reps))


def _split_w_in(w):
    o_gate_logits = 2 * A_WIDTH + 2 * B_QK + B_V + 4 * C_WIDTH
    o_gates = o_gate_logits + 4 * C_HEADS
    o_c = 2 * A_WIDTH + 2 * B_QK + B_V
    main = jnp.concatenate([w[:, o_gates:], w[:, o_c:o_c + 4 * C_WIDTH], w[:, :o_c]], axis=1)
    logit = jnp.pad(w[:, o_gate_logits:o_gates], ((0, 0), (0, GATE_TAB - 4 * C_HEADS)))
    return main.astype(BF16), logit.astype(BF16)


def _gate_row(v):
    return jnp.pad(v.astype(F32).reshape(1, 2 * C_HEADS), ((0, 0), (2 * C_HEADS, GATE_TAB - 4 * C_HEADS)))


def _layer(l, last, x, cx, mod, prm, tables, m_avg, eye_qk):
    (norm1_g, w_in, sgu_norm_g, sgu_w, sgu_b, w_a_br, q_norm_g, k_norm_g, lam_vecs, subln_g, w_b_br,
     conv_qkv_w, a_log, dt_bias, gdn_norm_g, w_c_br, w_o, norm2_g, w_up, conv_ffn_w, w_down) = prm
    bsz, n, d = x.shape
    lam_init = 0.8 - 0.6 * math.exp(-0.3 * l)
    w_main, w_logit = _split_w_in(w_in)
    mod_x = mod[:bsz].reshape(bsz, 1, 6, d)
    mod_c = jnp.broadcast_to(mod[bsz:bsz + 1].reshape(1, 1, 6, d), (bsz, 1, 6, d))
    n1 = norm1_g.reshape(1, d)
    n2 = norm2_g.reshape(1, d)
    qg = jnp.tile(q_norm_g.astype(F32), B_QK // B_HEAD_DIM).reshape(1, B_QK)
    kg = jnp.tile(k_norm_g.astype(F32), B_QK // B_HEAD_DIM).reshape(1, B_QK)
    sub_g = subln_g.astype(F32).reshape(-1, 1)
    gn = jnp.tile(gdn_norm_g.astype(F32), C_HEADS).reshape(1, C_WIDTH)
    alog_row, dtb_row = _gate_row(a_log), _gate_row(dt_bias)
    sgu_wb = sgu_w.astype(BF16)
    sgu_bt = sgu_b.astype(F32).T
    wa, wb, wc, wo = (t.astype(BF16) for t in (w_a_br, w_b_br, w_c_br, w_o))
    wu, wd = w_up.astype(BF16), w_down.astype(BF16)

    def project(t, m):
        return _norm_mod_matmul2(t, n1, m[:, :, 0], m[:, :, 1], w_main, w_logit, 1024, 1536, "in_proj")

    def gdn_pre(p, lg):
        q, k, kt, v, tab, row = _gdn_prep(p, lg, conv_qkv_w.astype(F32), alog_row, dtb_row, eye_qk, 256)
        uf, wf, ub, wb_ = _gdn_local(k, kt, v, tab, row, 512)
        return q, kt, uf, wf, ub, wb_, tab, row

    p_x, lg_x = project(x, mod_x)
    p_c, lg_c = project(cx, mod_c)

    eye_v = eye_qk[:B_V // B_HEADS, :B_V // B_HEADS]
    k_x, qt_x = _qk_prep(p_x, qg, kg, m_avg, eye_qk, tables, 512)
    k_c, qt_c = _qk_prep(p_c, qg, kg, m_avg, eye_qk, None, 512)
    yb_x = _diff_attention(lam_vecs, qt_x, k_c, p_c, k_x, p_x, sub_g, eye_v, lam_init, 1024)

    gx = gdn_pre(p_x, lg_x)
    gc = gdn_pre(p_c, lg_c)
    s0 = jnp.zeros((bsz, 2, C_HEADS, C_HEAD_DIM, C_HEAD_DIM), F32)
    of_c, ob_c, s_c = _gdn_scan(*gc, s0)
    of_x, ob_x, _ = _gdn_scan(*gx, s_c)

    def finish(t, p, yb, of, ob, m):
        ya = _spatial_gating(p, sgu_norm_g.astype(F32), sgu_wb, sgu_bt, 512)
        t = _merge(t, p, ya, yb, of, ob, gn, m[:, :, 2], wa, wb, wc, wo, 512)
        up = _norm_mod_matmul(t, n2, m[:, :, 3], m[:, :, 4], wu, BF16, 1024, 1408, "ffn_up")
        return _ffn_down(t, up, conv_ffn_w.astype(F32), wd, m[:, :, 5], 512, 1408)

    x = finish(x, p_x, yb_x, of_x, ob_x, mod_x)
    if not last:
        yb_c = _diff_attention(lam_vecs, qt_c, k_c, p_c, None, None, sub_g, eye_v, lam_init, 512)
        cx = finish(cx, p_c, yb_c, of_c, ob_c, mod_c)
    return x, cx


def kernel(x, c, ctx, c_ctx, w_mod, b_mod, norm1_g, w_in, sgu_norm_g, sgu_w, sgu_b, w_a_br, q_norm_g, k_norm_g, lambda_q1, lambda_k1, lambda_q2, lambda_k2, subln_g, w_b_br, conv_qkv_w, a_log, dt_bias, gdn_norm_g, w_c_br, w_o, norm2_g, w_up, conv_ffn_w, w_down):
    bsz, n, d = x.shape
    depth = w_mod.shape[0]
    tables = _rope_tables(n)
    seg = jnp.arange(B_QK) // B_HEAD_DIM
    m_avg = jnp.where(seg[:, None] == seg[None, :], 1.0 / B_HEAD_DIM, 0.0).astype(BF16)
    eye_qk = jnp.eye(B_QK, dtype=BF16)
    rows = bsz + 1
    pad = (-rows) % 8
    cs = jnp.concatenate([c, c_ctx[None, :], jnp.zeros((pad, d), F32)], axis=0)
    cx = ctx
    for l in range(depth):
        mod = _modulation(cs, w_mod[l], b_mod[l].reshape(1, -1))
        lam_vecs = jnp.pad(jnp.stack([lambda_q1[l], lambda_k1[l], lambda_q2[l], lambda_k2[l]]).astype(F32),
                           ((0, 4), (0, LANES - B_HEAD_DIM)))
        prm = (norm1_g[l], w_in[l], sgu_norm_g[l], sgu_w[l], sgu_b[l], w_a_br[l], q_norm_g[l], k_norm_g[l],
               lam_vecs, subln_g[l], w_b_br[l], conv_qkv_w[l], a_log[l], dt_bias[l], gdn_norm_g[l],
               w_c_br[l], w_o[l], norm2_g[l], w_up[l], conv_ffn_w[l], w_down[l])
        x, cx = _layer(l, l == depth - 1, x, cx, mod, prm, tables, m_avg, eye_qk)
    return x
```

```python
import functools
import math

import jax
import jax.numpy as jnp
from jax import lax
from jax.experimental import pallas as pl
from jax.experimental.pallas import tpu as pltpu

F32 = jnp.float32
BF16 = jnp.bfloat16
HIGHEST = lax.Precision.HIGHEST

GRID_W = 64
A_WIDTH = 512
A_GROUPS = 4
A_GROUP_DIM = A_WIDTH // A_GROUPS
A_CHUNK = 128
B_HEADS = 4
B_HEAD_DIM = 64
B_QK = B_HEADS * 2 * B_HEAD_DIM
B_V = B_HEADS * 2 * B_HEAD_DIM
ROPE_BASE = 10000.0
C_HEADS = 4
C_HEAD_DIM = 128
C_WIDTH = C_HEADS * C_HEAD_DIM
GDN_CHUNK = 64
N_BRANCHES = 3
EPS = 1e-6

LANES = 128
BF16_SUBLANES = 16
VMEM_LIMIT = 52 * 1024 * 1024

BLK = 512
COL_GATE = 0
COL_CQKV = 6
COL_CZ = 9
COL_U = 10
COL_SV = 11
COL_BQ = 12
COL_BK = 13
COL_BV = 14
N_MAIN = 15 * BLK
GATE_TAB = 128


def _cparams(sem):
    return pltpu.CompilerParams(dimension_semantics=sem, vmem_limit_bytes=VMEM_LIMIT)


def _mm(a, b):
    return jnp.dot(a.astype(BF16), b.astype(BF16), preferred_element_type=F32)


def _silu(x):
    return x * jax.nn.sigmoid(x)


def _mod_kernel(c_ref, w_ref, b_ref, o_ref):
    s = _silu(c_ref[...])
    o_ref[...] = jnp.dot(s, w_ref[...], preferred_element_type=F32, precision=HIGHEST) + b_ref[...]


def _modulation(cs, w, b):
    r, d = cs.shape
    n = w.shape[1]
    tn = 1536
    return pl.pallas_call(
        _mod_kernel,
        grid=(n // tn,),
        in_specs=[pl.BlockSpec((r, d), lambda j: (0, 0)),
                  pl.BlockSpec((d, tn), lambda j: (0, j)),
                  pl.BlockSpec((1, tn), lambda j: (0, j))],
        out_specs=pl.BlockSpec((r, tn), lambda j: (0, j)),
        out_shape=jax.ShapeDtypeStruct((r, n), F32),
        compiler_params=_cparams(("arbitrary",)),
        name="modulation",
    )(cs, w, b)


def _nmm2_kernel(x_ref, g_ref, sh_ref, sc_ref, w_ref, w2_ref, o_ref, o2_ref, h_ref):
    @pl.when(pl.program_id(2) == 0)
    def _():
        x = x_ref[0]
        y = x * lax.rsqrt(jnp.mean(x * x, axis=-1, keepdims=True) + EPS) * g_ref[...]
        h_ref[...] = (y * (1.0 + sc_ref[0]) + sh_ref[0]).astype(BF16)
        o2_ref[0] = jnp.dot(h_ref[...], w2_ref[...], preferred_element_type=F32)

    o_ref[0] = jnp.dot(h_ref[...], w_ref[...], preferred_element_type=F32).astype(o_ref.dtype)


def _norm_mod_matmul2(x, g, shift, scale, w, w2, tm, tn, name):
    bsz, n, d = x.shape
    nc, nc2 = w.shape[1], w2.shape[1]
    tm = min(tm, n)
    return pl.pallas_call(
        _nmm2_kernel,
        grid=(bsz, n // tm, nc // tn),
        in_specs=[pl.BlockSpec((1, tm, d), lambda b, i, j: (b, i, 0)),
                  pl.BlockSpec((1, d), lambda b, i, j: (0, 0)),
                  pl.BlockSpec((1, 1, d), lambda b, i, j: (b, 0, 0)),
                  pl.BlockSpec((1, 1, d), lambda b, i, j: (b, 0, 0)),
                  pl.BlockSpec((d, tn), lambda b, i, j: (0, j)),
                  pl.BlockSpec((d, nc2), lambda b, i, j: (0, 0))],
        out_specs=[pl.BlockSpec((1, tm, tn), lambda b, i, j: (b, i, j)),
                   pl.BlockSpec((1, tm, nc2), lambda b, i, j: (b, i, 0))],
        out_shape=[jax.ShapeDtypeStruct((bsz, n, nc), BF16),
                   jax.ShapeDtypeStruct((bsz, n, nc2), F32)],
        scratch_shapes=[pltpu.VMEM((tm, d), BF16)],
        compiler_params=_cparams(("parallel", "parallel", "arbitrary")),
        name=name,
    )(x, g, shift, scale, w, w2)


def _sgu_kernel(u_ref, v_ref, ng_ref, w_ref, b_ref, o_ref):
    tm = u_ref.shape[1]
    for c in range(tm // A_CHUNK):
        rows = slice(c * A_CHUNK, (c + 1) * A_CHUNK)
        for g in range(A_GROUPS):
            cols = slice(g * A_GROUP_DIM, (g + 1) * A_GROUP_DIM)
            u = jax.nn.gelu(u_ref[0, rows, cols].astype(F32))
            v = jax.nn.gelu(v_ref[0, rows, cols].astype(F32))
            vn = v * lax.rsqrt(jnp.mean(v * v, axis=-1, keepdims=True) + EPS) * ng_ref[g:g + 1, :]
            mixed = _mm(w_ref[g], vn) + b_ref[:, g:g + 1]
            o_ref[0, rows, cols] = (u * mixed).astype(o_ref.dtype)


def _spatial_gating(p, ng, w, b_t, tm):
    bsz, n, _ = p.shape
    tm = min(tm, n)
    return pl.pallas_call(
        _sgu_kernel,
        grid=(bsz, n // tm),
        in_specs=[pl.BlockSpec((1, tm, BLK), lambda b, i: (b, i, COL_U)),
                  pl.BlockSpec((1, tm, BLK), lambda b, i: (b, i, COL_SV)),
                  pl.BlockSpec(ng.shape, lambda b, i: (0, 0)),
                  pl.BlockSpec(w.shape, lambda b, i: (0, 0, 0)),
                  pl.BlockSpec(b_t.shape, lambda b, i: (0, 0))],
        out_specs=pl.BlockSpec((1, tm, A_WIDTH), lambda b, i: (b, i, 0)),
        out_shape=jax.ShapeDtypeStruct((bsz, n, A_WIDTH), BF16),
        compiler_params=_cparams(("parallel", "parallel")),
        name="spatial_gating",
    )(p, p, ng, w, b_t)


def _seg_rms(x, g, m_ref):
    x2 = x * x
    hi = x2.astype(BF16)
    lo = (x2 - hi.astype(F32)).astype(BF16)
    ms = (jnp.dot(hi, m_ref[...], preferred_element_type=F32)
          + jnp.dot(lo, m_ref[...], preferred_element_type=F32))
    return x * lax.rsqrt(ms + EPS) * g


def _rope(x, cos, sin_signed):
    half = B_HEAD_DIM // 4
    width = x.shape[1]
    lane = lax.broadcasted_iota(jnp.int32, x.shape, 1)
    first = (lane & (2 * half - 1)) < half
    partner = jnp.where(first, pltpu.roll(x, width - half, axis=1), pltpu.roll(x, half, axis=1))
    return x * cos + partner * sin_signed


def _transpose_store(x, eye_ref, o_ref):
    xt = lax.dot_general(eye_ref[...], x.astype(BF16), (((1,), (1,)), ((), ())), preferred_element_type=F32)
    o_ref[0] = xt.astype(o_ref.dtype)


Q_SCALE = B_HEAD_DIM ** -0.5 * math.log2(math.e)
ATTN_KEY_BLOCK = 512


def _qk_prep_rope_kernel(q_ref, k_ref, qg_ref, kg_ref, m_ref, eye_ref, cos_ref, sin_ref, ko_ref, qt_ref):
    cos = cos_ref[...]
    sin = sin_ref[...]
    q = _rope(_seg_rms(q_ref[0].astype(F32), qg_ref[...], m_ref), cos, sin)
    k = _rope(_seg_rms(k_ref[0].astype(F32), kg_ref[...], m_ref), cos, sin)
    ko_ref[0] = k.astype(ko_ref.dtype)
    _transpose_store(q * Q_SCALE, eye_ref, qt_ref)


def _qk_prep_plain_kernel(q_ref, k_ref, qg_ref, kg_ref, m_ref, eye_ref, ko_ref, qt_ref):
    q = _seg_rms(q_ref[0].astype(F32), qg_ref[...], m_ref)
    k = _seg_rms(k_ref[0].astype(F32), kg_ref[...], m_ref)
    ko_ref[0] = k.astype(ko_ref.dtype)
    _transpose_store(q * Q_SCALE, eye_ref, qt_ref)


def _qk_prep(p, qg, kg, m_avg, eye, tables, tm):
    bsz, n, _ = p.shape
    tm = min(tm, n)
    in_specs = [pl.BlockSpec((1, tm, BLK), lambda b, i: (b, i, COL_BQ)),
                pl.BlockSpec((1, tm, BLK), lambda b, i: (b, i, COL_BK)),
                pl.BlockSpec((1, B_QK), lambda b, i: (0, 0)),
                pl.BlockSpec((1, B_QK), lambda b, i: (0, 0)),
                pl.BlockSpec((B_QK, B_QK), lambda b, i: (0, 0)),
                pl.BlockSpec((B_QK, B_QK), lambda b, i: (0, 0))]
    args = [p, p, qg, kg, m_avg, eye]
    if tables is None:
        body = _qk_prep_plain_kernel
    else:
        body = _qk_prep_rope_kernel
        in_specs += [pl.BlockSpec((tm, B_QK), lambda b, i: (i, 0)),
                     pl.BlockSpec((tm, B_QK), lambda b, i: (i, 0))]
        args += list(tables)
    return pl.pallas_call(
        body,
        grid=(bsz, n // tm),
        in_specs=in_specs,
        out_specs=[pl.BlockSpec((1, tm, B_QK), lambda b, i: (b, i, 0)),
                   pl.BlockSpec((1, B_QK, tm), lambda b, i: (b, 0, i))],
        out_shape=[jax.ShapeDtypeStruct((bsz, n, B_QK), BF16),
                   jax.ShapeDtypeStruct((bsz, B_QK, n), BF16)],
        compiler_params=_cparams(("parallel", "parallel")),
        name="qk_prep",
    )(*args)


def _attn_kernel(*refs, has_lat, lam_init):
    if has_lat:
        lam_ref, qt_ref, kc_ref, vc_ref, kx_ref, vx_ref, g_ref, eye_ref, o_ref, vtc_s, vtx_s = refs
    else:
        lam_ref, qt_ref, kc_ref, vc_ref, g_ref, eye_ref, o_ref, vtc_s = refs
    hv = vc_ref.shape[2]

    @pl.when(pl.program_id(2) == 0)
    def _():
        pairs = ((vc_ref, vtc_s), (vx_ref, vtx_s)) if has_lat else ((vc_ref, vtc_s),)
        for v_ref, vt_s in pairs:
            vt = lax.dot_general(eye_ref[...], v_ref[0], (((1,), (1,)), ((), ())), preferred_element_type=F32)
            vt_s[:hv, :] = vt.astype(BF16)
            vt_s[hv:, :] = jnp.ones((BF16_SUBLANES, vt_s.shape[1]), BF16)

    qt = qt_ref[0]
    row = lax.broadcasted_iota(jnp.int32, qt.shape, 0)
    zero = jnp.zeros_like(qt)
    kblk = min(ATTN_KEY_BLOCK, kx_ref.shape[1]) if has_lat else 0
    assert not has_lat or kx_ref.shape[1] % kblk == 0
    n_blk = kx_ref.shape[1] // kblk if has_lat else 0
    blk = lambda j: slice((j - 1) * kblk, j * kblk)
    keys = lambda j: kc_ref[0] if j == 0 else kx_ref[0, blk(j), :]
    vals_t = lambda j: vtc_s[...] if j == 0 else vtx_s[:, blk(j)]
    q_sub = [jnp.where(row < B_HEAD_DIM, qt, zero), jnp.where(row >= B_HEAD_DIM, qt, zero)]
    score = lambda m, j: jnp.dot(keys(j), q_sub[m], preferred_element_type=F32)
    pv = lambda j, p: jnp.dot(vals_t(j), p, preferred_element_type=F32)

    m_run, acc = [None, None], [None, None]
    s_next = [score(m, 0) for m in range(2)]
    for j in range(n_blk + 1):
        s_cur = s_next
        if j < n_blk:
            s_next = [score(m, j + 1) for m in range(2)]
        for m in range(2):
            s = s_cur[m]
            blk_max = jnp.max(s, axis=0, keepdims=True)
            if j == 0:
                m_new = blk_max
                acc[m] = pv(j, jnp.exp2(s - m_new).astype(BF16))
            else:
                m_new = jnp.maximum(m_run[m], blk_max)
                alpha = jnp.exp2(m_run[m] - m_new)
                acc[m] = alpha * acc[m] + pv(j, jnp.exp2(s - m_new).astype(BF16))
            m_run[m] = m_new
    outs = [a[:hv] / a[hv:hv + 1] for a in acc]
    lv = lam_ref[...]
    lam = (jnp.exp(jnp.sum(lv[0:1] * lv[1:2], axis=-1, keepdims=True))
           - jnp.exp(jnp.sum(lv[2:3] * lv[3:4], axis=-1, keepdims=True)) + lam_init)
    o = outs[0] - lam * outs[1]
    o = o * lax.rsqrt(jnp.mean(o * o, axis=0, keepdims=True) + EPS) * g_ref[...]
    o_ref[0] = (o * (1.0 - lam_init)).T.astype(o_ref.dtype)


def _diff_attention(lam_vecs, qt, kc, pc, kx, px, subln_g, eye, lam_init, tq):
    bsz, _, n = qt.shape
    nctx = kc.shape[1]
    tq = min(tq, n)
    hv = B_V // B_HEADS
    vcol = COL_BV * (BLK // hv)
    has_lat = kx is not None
    in_specs = [pl.BlockSpec(lam_vecs.shape, lambda b, h, i: (0, 0)),
                pl.BlockSpec((1, hv, tq), lambda b, h, i: (b, h, i)),
                pl.BlockSpec((1, nctx, hv), lambda b, h, i: (b, 0, h)),
                pl.BlockSpec((1, nctx, hv), lambda b, h, i: (b, 0, vcol + h))]
    args = [lam_vecs, qt, kc, pc]
    scratch = [pltpu.VMEM((hv + BF16_SUBLANES, nctx), BF16)]
    if has_lat:
        nlat = kx.shape[1]
        in_specs += [pl.BlockSpec((1, nlat, hv), lambda b, h, i: (b, 0, h)),
                     pl.BlockSpec((1, nlat, hv), lambda b, h, i: (b, 0, vcol + h))]
        args += [kx, px]
        scratch.append(pltpu.VMEM((hv + BF16_SUBLANES, nlat), BF16))
    in_specs += [pl.BlockSpec((hv, 1), lambda b, h, i: (0, 0)),
                 pl.BlockSpec((hv, hv), lambda b, h, i: (0, 0))]
    args += [subln_g, eye]
    return pl.pallas_call(
        functools.partial(_attn_kernel, has_lat=has_lat, lam_init=lam_init),
        grid=(bsz, B_HEADS, n // tq),
        in_specs=in_specs,
        out_specs=pl.BlockSpec((1, tq, hv), lambda b, h, i: (b, i, h)),
        out_shape=jax.ShapeDtypeStruct((bsz, n, B_V), BF16),
        scratch_shapes=scratch,
        compiler_params=_cparams(("parallel", "parallel", "arbitrary")),
        name="diff_attention",
    )(*args)


def _conv3(cur, prev_row, next_row, w):
    t = cur.shape[0]
    rid = lax.broadcasted_iota(jnp.int32, cur.shape, 0)
    xm = jnp.where(rid == 0, prev_row, pltpu.roll(cur, 1, axis=0))
    xp = jnp.where(rid == t - 1, next_row, pltpu.roll(cur, t - 1, axis=0))
    return w[0:1] * xm + w[1:2] * cur + w[2:3] * xp


def _halo_rows(prev_ref, next_ref, i, n_tiles):
    h = prev_ref.shape[1]
    prev_row = prev_ref[0, h - 1:h, :].astype(F32)
    next_row = next_ref[0, 0:1, :].astype(F32)
    prev_row = jnp.where(i > 0, prev_row, jnp.zeros_like(prev_row))
    next_row = jnp.where(i < n_tiles - 1, next_row, jnp.zeros_like(next_row))
    return prev_row, next_row


def _halo_specs(tm, width, col, n):
    r = tm // BF16_SUBLANES
    last = n // BF16_SUBLANES - 1
    cur = pl.BlockSpec((1, tm, width), lambda b, i, *_: (b, i, col(*_)))
    prev = pl.BlockSpec((1, BF16_SUBLANES, width),
                        lambda b, i, *_: (b, jnp.maximum(i * r - 1, 0), col(*_)))
    nxt = pl.BlockSpec((1, BF16_SUBLANES, width),
                       lambda b, i, *_: (b, jnp.minimum((i + 1) * r, last), col(*_)))
    return cur, prev, nxt


def _gdn_prep_kernel(x_ref, xp_ref, xn_ref, cw_ref, lg_ref, alog_ref, dtb_ref, eye_ref,
                     q_ref, k_ref, kt_ref, v_ref, tab_ref, row_ref):
    i = pl.program_id(1)
    tm = x_ref.shape[1]
    prev_row, next_row = _halo_rows(xp_ref, xn_ref, i, pl.num_programs(1))
    y = _silu(_conv3(x_ref[0].astype(F32), prev_row, next_row, cw_ref[...]))
    for h in range(C_HEADS):
        cq = slice(h * C_HEAD_DIM, (h + 1) * C_HEAD_DIM)
        ck = slice(C_WIDTH + h * C_HEAD_DIM, C_WIDTH + (h + 1) * C_HEAD_DIM)
        q = y[:, cq]
        k = y[:, ck]
        qn = q * lax.rsqrt(jnp.sum(q * q, axis=-1, keepdims=True) + EPS)
        kn = k * lax.rsqrt(jnp.sum(k * k, axis=-1, keepdims=True) + EPS)
        q_ref[0, :, cq] = (qn * C_HEAD_DIM ** -0.5).astype(q_ref.dtype)
        k_ref[0, :, cq] = kn.astype(k_ref.dtype)
    _transpose_store(k_ref[0], eye_ref, kt_ref)
    v_ref[0] = y[:, 2 * C_WIDTH:].astype(v_ref.dtype)

    lg = lg_ref[0]
    col = lax.broadcasted_iota(jnp.int32, lg.shape, 1)
    beta = jax.nn.sigmoid(lg)
    z = lg + dtb_ref[...]
    softplus = jnp.maximum(z, 0.0) + jnp.log(1.0 + jnp.exp(-jnp.abs(z)))
    g = -jnp.exp(alog_ref[...]) * softplus
    g = jnp.where((col >= 2 * C_HEADS) & (col < 4 * C_HEADS), g, 0.0)
    ri = lax.broadcasted_iota(jnp.int32, (tm, tm), 0)
    ci = lax.broadcasted_iota(jnp.int32, (tm, tm), 1)
    same = lax.shift_right_logical(ri, 6) == lax.shift_right_logical(ci, 6)
    lower = jnp.where(same & (ci <= ri), 1.0, 0.0).astype(F32)
    upper = jnp.where(same & (ci >= ri), 1.0, 0.0).astype(F32)
    cum_f = jnp.dot(lower, g, preferred_element_type=F32, precision=HIGHEST)
    cum_b = jnp.dot(upper, g, preferred_element_type=F32, precision=HIGHEST)
    cum = jnp.where(col >= 3 * C_HEADS, cum_b, cum_f)
    tab = jnp.where(col < 2 * C_HEADS, beta, cum)
    tab_ref[0] = tab
    sel = (lax.broadcasted_iota(jnp.int32, (4 * C_HEADS, GATE_TAB), 0)
           == lax.broadcasted_iota(jnp.int32, (4 * C_HEADS, GATE_TAB), 1)).astype(F32)
    row_ref[0] = lax.dot_general(sel, tab, (((1,), (1,)), ((), ())), preferred_element_type=F32,
                                 precision=HIGHEST)


def _gdn_prep(p, logits, conv_w, alog_row, dtb_row, eye, tm):
    bsz, n, _ = p.shape
    tm = min(tm, n)
    cur, prev, nxt = _halo_specs(tm, 3 * C_WIDTH, lambda: COL_CQKV // 3, n)
    bf = jax.ShapeDtypeStruct((bsz, n, C_WIDTH), BF16)
    tok = pl.BlockSpec((1, tm, C_WIDTH), lambda b, i: (b, i, 0))
    return pl.pallas_call(
        _gdn_prep_kernel,
        grid=(bsz, n // tm),
        in_specs=[cur, prev, nxt,
                  pl.BlockSpec(conv_w.shape, lambda b, i: (0, 0)),
                  pl.BlockSpec((1, tm, GATE_TAB), lambda b, i: (b, i, 0)),
                  pl.BlockSpec((1, GATE_TAB), lambda b, i: (0, 0)),
                  pl.BlockSpec((1, GATE_TAB), lambda b, i: (0, 0)),
                  pl.BlockSpec(eye.shape, lambda b, i: (0, 0))],
        out_specs=[tok, tok,
                   pl.BlockSpec((1, C_WIDTH, tm), lambda b, i: (b, 0, i)),
                   tok,
                   pl.BlockSpec((1, tm, GATE_TAB), lambda b, i: (b, i, 0)),
                   pl.BlockSpec((1, 4 * C_HEADS, tm), lambda b, i: (b, 0, i))],
        out_shape=[bf, bf, jax.ShapeDtypeStruct((bsz, C_WIDTH, n), BF16), bf,
                   jax.ShapeDtypeStruct((bsz, n, GATE_TAB), F32),
                   jax.ShapeDtypeStruct((bsz, 4 * C_HEADS, n), F32)],
        compiler_params=_cparams(("parallel", "parallel")),
        name="gdn_prep",
    )(p, p, p, conv_w, logits, alog_row, dtb_row, eye)


def _chunk_masks(d):
    ri = lax.broadcasted_iota(jnp.int32, (GDN_CHUNK, GDN_CHUNK), 0)
    ci = lax.broadcasted_iota(jnp.int32, (GDN_CHUNK, GDN_CHUNK), 1)
    if d == 0:
        return ri >= ci, ri > ci
    return ri <= ci, ri < ci


def _gate_views(tab, row, rows, d, h):
    beta = tab[rows, d * C_HEADS + h:d * C_HEADS + h + 1]
    gcol = tab[rows, 2 * C_HEADS + d * C_HEADS + h:2 * C_HEADS + d * C_HEADS + h + 1]
    grow = row[2 * C_HEADS + d * C_HEADS + h:2 * C_HEADS + d * C_HEADS + h + 1, rows]
    return beta, gcol, grow


def _gdn_local_kernel(k_ref, kt_ref, v_ref, tab_ref, row_ref, uf_ref, wf_ref, ub_ref, wb_ref):
    tl = k_ref.shape[1]
    tab = tab_ref[0]
    row = row_ref[0]
    nh = C_HEADS
    cw = nh * GDN_CHUNK
    r4 = lax.broadcasted_iota(jnp.int32, (cw, cw), 0)
    c4 = lax.broadcasted_iota(jnp.int32, (cw, cw), 1)
    bd_mask = lax.shift_right_logical(r4, 6) == lax.shift_right_logical(c4, 6)
    r64 = lax.broadcasted_iota(jnp.int32, (GDN_CHUNK, cw), 0)
    c64 = lax.broadcasted_iota(jnp.int32, (GDN_CHUNK, cw), 1) & (GDN_CHUNK - 1)
    eye_cat = jnp.where(r64 == c64, 1.0, 0.0)
    bd_zero = jnp.zeros((cw, cw), BF16)

    def block_diag(b):
        return jnp.where(bd_mask, jnp.concatenate([b.astype(BF16)] * nh, axis=0), bd_zero)

    def off_block(lg):
        shr = lax.shift_right_logical
        return (shr(r64, lg) != shr(c64, lg)) & (shr(r64, lg + 1) == shr(c64, lg + 1))

    out_refs = ((uf_ref, wf_ref), (ub_ref, wb_ref))
    units = [(c, d) for c in range(tl // GDN_CHUNK) for d in range(2)]
    a_cat, rhs = [], []
    for c, d in units:
        rows = slice(c * GDN_CHUNK, (c + 1) * GDN_CHUNK)
        _, strict = _chunk_masks(d)
        a_list, rhs_list = [], []
        for h in range(nh):
            cols = slice(h * C_HEAD_DIM, (h + 1) * C_HEAD_DIM)
            k = k_ref[0, rows, cols].astype(F32)
            v = v_ref[0, rows, cols].astype(F32)
            beta, gcol, grow = _gate_views(tab, row, rows, d, h)
            seg = jnp.exp(jnp.minimum(gcol - grow, 0.0))
            kb = k * beta
            a_list.append(jnp.where(strict, _mm(kb, kt_ref[0, cols, rows]) * seg, 0.0))
            rhs_list.append(jnp.concatenate([v * beta, kb * jnp.exp(gcol)], axis=1).astype(BF16))
        a_cat.append(jnp.concatenate(a_list, axis=1))
        rhs.append(rhs_list)
    t = [eye_cat - jnp.where(off_block(0), a, 0.0) for a in a_cat]
    for lg in range(1, 6):
        x = [_mm(jnp.where(off_block(lg), a, 0.0), block_diag(tu)) for a, tu in zip(a_cat, t)]
        t = [tu - _mm(tu, block_diag(xu)) for tu, xu in zip(t, x)]
    for (c, d), tu, rhs_list in zip(units, t, rhs):
        rows = slice(c * GDN_CHUNK, (c + 1) * GDN_CHUNK)
        u_ref, w_ref = out_refs[d]
        for h in range(nh):
            cols = slice(h * C_HEAD_DIM, (h + 1) * C_HEAD_DIM)
            uw = _mm(tu[:, h * GDN_CHUNK:(h + 1) * GDN_CHUNK], rhs_list[h])
            u_ref[0, rows, cols] = uw[:, :C_HEAD_DIM].astype(u_ref.dtype)
            w_ref[0, rows, cols] = uw[:, C_HEAD_DIM:].astype(w_ref.dtype)


def _gdn_local(k, kt, v, tab, row, tl):
    bsz, n, _ = k.shape
    tl = min(tl, n)
    tok = pl.BlockSpec((1, tl, C_WIDTH), lambda b, i: (b, i, 0))
    out = jax.ShapeDtypeStruct((bsz, n, C_WIDTH), BF16)
    return pl.pallas_call(
        _gdn_local_kernel,
        grid=(bsz, n // tl),
        in_specs=[tok,
                  pl.BlockSpec((1, C_WIDTH, tl), lambda b, i: (b, 0, i)),
                  tok,
                  pl.BlockSpec((1, tl, GATE_TAB), lambda b, i: (b, i, 0)),
                  pl.BlockSpec((1, 4 * C_HEADS, tl), lambda b, i: (b, 0, i))],
        out_specs=[tok, tok, tok, tok],
        out_shape=[out, out, out, out],
        compiler_params=_cparams(("parallel", "parallel")),
        name="gdn_local",
    )(k, kt, v, tab, row)


SCAN_TILE = 2 * GDN_CHUNK


def _gdn_scan_kernel(qf_ref, ktf_ref, uf_ref, wf_ref, tabf_ref, rowf_ref,
                     qb_ref, ktb_ref, ub_ref, wb_ref, tabb_ref, rowb_ref, s0_ref,
                     of_ref, ob_ref, sfin_ref, s_ref):
    step = pl.program_id(1)

    @pl.when(step == 0)
    def _():
        s_ref[...] = s0_ref[0]

    dirs = ((qf_ref, ktf_ref, uf_ref, wf_ref, tabf_ref, rowf_ref, of_ref),
            (qb_ref, ktb_ref, ub_ref, wb_ref, tabb_ref, rowb_ref, ob_ref))
    n_chunks = SCAN_TILE // GDN_CHUNK
    chains = [(d, h) for d in range(2) for h in range(C_HEADS)]
    tabs = [r[4][0] for r in dirs]
    rowtabs = [r[5][0] for r in dirs]
    for sub in range(n_chunks):
        local = []
        for d, h in chains:
            q_ref, kt_ref, u_ref, w_ref, _, _, _ = dirs[d]
            c = sub if d == 0 else n_chunks - 1 - sub
            rows = slice(c * GDN_CHUNK, (c + 1) * GDN_CHUNK)
            cols = slice(h * C_HEAD_DIM, (h + 1) * C_HEAD_DIM)
            incl, _ = _chunk_masks(d)
            q = q_ref[0, rows, cols].astype(F32)
            kt = kt_ref[0, cols, rows].astype(F32)
            _, gcol, grow = _gate_views(tabs[d], rowtabs[d], rows, d, h)
            glast = gcol[GDN_CHUNK - 1:GDN_CHUNK] if d == 0 else gcol[0:1]
            seg = jnp.where(incl, jnp.exp(jnp.minimum(gcol - grow, 0.0)), 0.0)
            intra = (_mm(q, kt) * seg).astype(BF16)
            qd = (q * jnp.exp(gcol)).astype(BF16)
            kt_tail = (kt * jnp.exp(glast - grow)).astype(BF16)
            lhs = jnp.concatenate([w_ref[0, rows, cols], qd], axis=0)
            local.append((rows, cols, lhs, intra, kt_tail, jnp.exp(glast)))
        states = [s_ref[d, h] for d, h in chains]
        ws_qs = [_mm(loc[2], s) for loc, s in zip(local, states)]
        for (d, h), loc, s, wq in zip(chains, local, states, ws_qs):
            rows, cols, _, intra, kt_tail, gtot = loc
            u_ref, o_ref = dirs[d][2], dirs[d][6]
            v_new = (u_ref[0, rows, cols].astype(F32) - wq[:GDN_CHUNK]).astype(BF16)
            o_ref[0, rows, cols] = wq[GDN_CHUNK:] + jnp.dot(intra, v_new, preferred_element_type=F32)
            s_ref[d, h] = s * gtot + jnp.dot(kt_tail, v_new, preferred_element_type=F32)

    @pl.when(step == pl.num_programs(1) - 1)
    def _():
        sfin_ref[0] = s_ref[...]


def _gdn_scan(q, kt, uf, wf, ub, wb, tab, row, s0):
    bsz, n, _ = q.shape
    nt = n // SCAN_TILE
    tok_f = pl.BlockSpec((1, SCAN_TILE, C_WIDTH), lambda b, i: (b, i, 0))
    tok_b = pl.BlockSpec((1, SCAN_TILE, C_WIDTH), lambda b, i: (b, nt - 1 - i, 0))
    ktr_f = pl.BlockSpec((1, C_WIDTH, SCAN_TILE), lambda b, i: (b, 0, i))
    ktr_b = pl.BlockSpec((1, C_WIDTH, SCAN_TILE), lambda b, i: (b, 0, nt - 1 - i))
    tab_f = pl.BlockSpec((1, SCAN_TILE, GATE_TAB), lambda b, i: (b, i, 0))
    tab_b = pl.BlockSpec((1, SCAN_TILE, GATE_TAB), lambda b, i: (b, nt - 1 - i, 0))
    row_f = pl.BlockSpec((1, 4 * C_HEADS, SCAN_TILE), lambda b, i: (b, 0, i))
    row_b = pl.BlockSpec((1, 4 * C_HEADS, SCAN_TILE), lambda b, i: (b, 0, nt - 1 - i))
    st = pl.BlockSpec((1, 2, C_HEADS, C_HEAD_DIM, C_HEAD_DIM), lambda b, i: (b, 0, 0, 0, 0))
    o_sds = jax.ShapeDtypeStruct((bsz, n, C_WIDTH), F32)
    return pl.pallas_call(
        _gdn_scan_kernel,
        grid=(bsz, nt),
        in_specs=[tok_f, ktr_f, tok_f, tok_f, tab_f, row_f,
                  tok_b, ktr_b, tok_b, tok_b, tab_b, row_b, st],
        out_specs=[tok_f, tok_b, st],
        out_shape=[o_sds, o_sds, jax.ShapeDtypeStruct(s0.shape, F32)],
        scratch_shapes=[pltpu.VMEM((2, C_HEADS, C_HEAD_DIM, C_HEAD_DIM), F32)],
        compiler_params=_cparams(("parallel", "arbitrary")),
        name="gdn_scan",
    )(q, kt, uf, wf, tab, row, q, kt, ub, wb, tab, row, s0)


def _merge_kernel(x_ref, g0_ref, g1_ref, g2_ref, ya_ref, yb_ref, of_ref, ob_ref, z_ref, gn_ref,
                  gt_ref, wa_ref, wb_ref, wc_ref, wo_ref, o_ref):
    o = of_ref[0] + ob_ref[0]
    z = z_ref[0].astype(F32)
    parts = []
    for h in range(C_HEADS):
        cols = slice(h * C_HEAD_DIM, (h + 1) * C_HEAD_DIM)
        oh = o[:, cols]
        parts.append(oh * lax.rsqrt(jnp.mean(oh * oh, axis=-1, keepdims=True) + EPS))
    yc = jnp.concatenate(parts, axis=1) * gn_ref[...] * _silu(z)
    y = (jax.nn.sigmoid(g0_ref[0].astype(F32)) * jnp.dot(ya_ref[0], wa_ref[...], preferred_element_type=F32)
         + jax.nn.sigmoid(g1_ref[0].astype(F32)) * jnp.dot(yb_ref[0], wb_ref[...], preferred_element_type=F32)
         + jax.nn.sigmoid(g2_ref[0].astype(F32)) * _mm(yc, wc_ref[...]))
    o_ref[0] = x_ref[0] + gt_ref[0] * _mm(y, wo_ref[...])


def _merge(x, p, ya, yb, of, ob, gn, gt, wa, wb, wc, wo, tm):
    bsz, n, d = x.shape
    tm = min(tm, n)
    tok = lambda width, col: pl.BlockSpec((1, tm, width), lambda b, i: (b, i, col))
    full = lambda a: pl.BlockSpec(a.shape, lambda b, i: (0,) * a.ndim)
    return pl.pallas_call(
        _merge_kernel,
        grid=(bsz, n // tm),
        in_specs=[tok(d, 0), tok(d, 0), tok(d, 1), tok(d, 2),
                  tok(A_WIDTH, 0), tok(B_V, 0), tok(C_WIDTH, 0), tok(C_WIDTH, 0), tok(BLK, COL_CZ),
                  full(gn), pl.BlockSpec((1, 1, d), lambda b, i: (b, 0, 0)),
                  full(wa), full(wb), full(wc), full(wo)],
        out_specs=tok(d, 0),
        out_shape=jax.ShapeDtypeStruct(x.shape, F32),
        compiler_params=_cparams(("parallel", "parallel")),
        name="merge",
    )(x, p, p, p, ya, yb, of, ob, p, gn, gt, wa, wb, wc, wo)


def _conv_ffn_kernel(x_ref, xp_ref, xn_ref, g_ref, sh_ref, sc_ref, gt_ref, wg_ref, wv_ref, cwg_ref,
                     cwv_ref, wd_ref, o_ref, h_ref, acc_ref):
    i = pl.program_id(1)
    kk = pl.program_id(2)
    nt = pl.num_programs(1)
    tm = x_ref.shape[1]
    hal = xp_ref.shape[1]

    @pl.when(kk == 0)
    def _():
        def norm_mod(x):
            y = x * lax.rsqrt(jnp.mean(x * x, axis=-1, keepdims=True) + EPS) * g_ref[...]
            return y * (1.0 + sc_ref[0]) + sh_ref[0]

        hp = norm_mod(xp_ref[0])
        hn = norm_mod(xn_ref[0])
        h_ref[0:hal, :] = jnp.where(i > 0, hp, jnp.zeros_like(hp)).astype(BF16)
        h_ref[hal:hal + tm, :] = norm_mod(x_ref[0]).astype(BF16)
        h_ref[hal + tm:, :] = jnp.where(i < nt - 1, hn, jnp.zeros_like(hn)).astype(BF16)
        acc_ref[...] = jnp.zeros(acc_ref.shape, F32)

    h = h_ref[...]

    def branch(w_ref, cw_ref):
        up = jnp.dot(h, w_ref[...], preferred_element_type=F32)
        rows = up.shape[0]
        cw = cw_ref[...]
        y = cw[0:1] * pltpu.roll(up, 1, axis=0) + cw[1:2] * up + cw[2:3] * pltpu.roll(up, rows - 1, axis=0)
        return y[hal:hal + tm]

    gate = branch(wg_ref, cwg_ref)
    val = branch(wv_ref, cwv_ref)
    acc_ref[...] += _mm(_silu(gate) * val, wd_ref[...])

    @pl.when(kk == pl.num_programs(2) - 1)
    def _():
        o_ref[0] = x_ref[0] + gt_ref[0] * acc_ref[...]


def _conv_ffn(x, g, shift, scale, gt, wu, conv_w, wd, tm, tk):
    bsz, n, d = x.shape
    f = wd.shape[0]
    tm = min(tm, n)
    nk = f // tk
    hal = BF16_SUBLANES
    x_cur, x_prev, x_next = _halo_specs(tm, d, lambda k: 0, n)
    vec = pl.BlockSpec((1, 1, d), lambda b, i, k: (b, 0, 0))
    return pl.pallas_call(
        _conv_ffn_kernel,
        grid=(bsz, n // tm, nk),
        in_specs=[x_cur, x_prev, x_next,
                  pl.BlockSpec((1, d), lambda b, i, k: (0, 0)),
                  vec, vec, vec,
                  pl.BlockSpec((d, tk), lambda b, i, k: (0, k)),
                  pl.BlockSpec((d, tk), lambda b, i, k: (0, nk + k)),
                  pl.BlockSpec((3, tk), lambda b, i, k: (0, k)),
                  pl.BlockSpec((3, tk), lambda b, i, k: (0, nk + k)),
                  pl.BlockSpec((tk, d), lambda b, i, k: (k, 0))],
        out_specs=pl.BlockSpec((1, tm, d), lambda b, i, k: (b, i, 0)),
        out_shape=jax.ShapeDtypeStruct(x.shape, F32),
        scratch_shapes=[pltpu.VMEM((tm + 2 * hal, d), BF16),
                        pltpu.VMEM((tm, d), F32)],
        compiler_params=_cparams(("parallel", "parallel", "arbitrary")),
        name="conv_ffn",
    )(x, x, x, g, shift, scale, gt, wu, wu, conv_w, conv_w, wd)


def _rope_tables(n):
    n_freq = B_HEAD_DIM // 4
    inv_freq = ROPE_BASE ** (-jnp.arange(n_freq, dtype=F32) / n_freq)
    pos = jnp.arange(n, dtype=jnp.int32)
    row = (pos // GRID_W).astype(F32)
    col = (pos % GRID_W).astype(F32)
    ang_r = row[:, None] * inv_freq
    ang_c = col[:, None] * inv_freq
    ang = jnp.concatenate([ang_r, ang_r, ang_c, ang_c], axis=-1)
    sign = jnp.tile(jnp.concatenate([-jnp.ones(n_freq, F32), jnp.ones(n_freq, F32)]), 2)
    reps = B_QK // B_HEAD_DIM
    return jnp.tile(jnp.cos(ang), (1, reps)), jnp.tile(jnp.sin(ang) * sign, (1, reps))


def _split_w_in(w):
    o_gate_logits = 2 * A_WIDTH + 2 * B_QK + B_V + 4 * C_WIDTH
    o_gates = o_gate_logits + 4 * C_HEADS
    o_c = 2 * A_WIDTH + 2 * B_QK + B_V
    main = jnp.concatenate([w[:, o_gates:], w[:, o_c:o_c + 4 * C_WIDTH], w[:, :o_c]], axis=1)
    logit = jnp.pad(w[:, o_gate_logits:o_gates], ((0, 0), (0, GATE_TAB - 4 * C_HEADS)))
    return main.astype(BF16), logit.astype(BF16)


def _gate_row(v):
    return jnp.pad(v.astype(F32).reshape(1, 2 * C_HEADS), ((0, 0), (2 * C_HEADS, GATE_TAB - 4 * C_HEADS)))


def _layer(l, last, x, cx, mod, prm, tables, m_avg, eye_qk):
    (norm1_g, w_in, sgu_norm_g, sgu_w, sgu_b, w_a_br, q_norm_g, k_norm_g, lam_vecs, subln_g, w_b_br,
     conv_qkv_w, a_log, dt_bias, gdn_norm_g, w_c_br, w_o, norm2_g, w_up, conv_ffn_w, w_down) = prm
    bsz, n, d = x.shape
    lam_init = 0.8 - 0.6 * math.exp(-0.3 * l)
    w_main, w_logit = _split_w_in(w_in)
    mod_x = mod[:bsz].reshape(bsz, 1, 6, d)
    mod_c = jnp.broadcast_to(mod[bsz:bsz + 1].reshape(1, 1, 6, d), (bsz, 1, 6, d))
    n1 = norm1_g.reshape(1, d)
    n2 = norm2_g.reshape(1, d)
    qg = jnp.tile(q_norm_g.astype(F32), B_QK // B_HEAD_DIM).reshape(1, B_QK)
    kg = jnp.tile(k_norm_g.astype(F32), B_QK // B_HEAD_DIM).reshape(1, B_QK)
    sub_g = subln_g.astype(F32).reshape(-1, 1)
    gn = jnp.tile(gdn_norm_g.astype(F32), C_HEADS).reshape(1, C_WIDTH)
    alog_row, dtb_row = _gate_row(a_log), _gate_row(dt_bias)
    sgu_wb = sgu_w.astype(BF16)
    sgu_bt = sgu_b.astype(F32).T
    wa, wb, wc, wo = (t.astype(BF16) for t in (w_a_br, w_b_br, w_c_br, w_o))
    wu, wd = w_up.astype(BF16), w_down.astype(BF16)

    def project(t, m):
        return _norm_mod_matmul2(t, n1, m[:, :, 0], m[:, :, 1], w_main, w_logit, 1024, 1536, "in_proj")

    def gdn_pre(p, lg):
        q, k, kt, v, tab, row = _gdn_prep(p, lg, conv_qkv_w.astype(F32), alog_row, dtb_row, eye_qk, 256)
        uf, wf, ub, wb_ = _gdn_local(k, kt, v, tab, row, 512)
        return q, kt, uf, wf, ub, wb_, tab, row

    p_x, lg_x = project(x, mod_x)
    p_c, lg_c = project(cx, mod_c)

    eye_v = eye_qk[:B_V // B_HEADS, :B_V // B_HEADS]
    k_x, qt_x = _qk_prep(p_x, qg, kg, m_avg, eye_qk, tables, 512)
    k_c, qt_c = _qk_prep(p_c, qg, kg, m_avg, eye_qk, None, 512)
    yb_x = _diff_attention(lam_vecs, qt_x, k_c, p_c, k_x, p_x, sub_g, eye_v, lam_init, 1024)

    gx = gdn_pre(p_x, lg_x)
    gc = gdn_pre(p_c, lg_c)
    s0 = jnp.zeros((bsz, 2, C_HEADS, C_HEAD_DIM, C_HEAD_DIM), F32)
    of_c, ob_c, s_c = _gdn_scan(*gc, s0)
    of_x, ob_x, _ = _gdn_scan(*gx, s_c)

    def finish(t, p, yb, of, ob, m):
        ya = _spatial_gating(p, sgu_norm_g.astype(F32), sgu_wb, sgu_bt, 512)
        t = _merge(t, p, ya, yb, of, ob, gn, m[:, :, 2], wa, wb, wc, wo, 512)
        return _conv_ffn(t, n2, m[:, :, 3], m[:, :, 4], m[:, :, 5], wu, conv_ffn_w.astype(F32), wd, 1024, 1408)

    x = finish(x, p_x, yb_x, of_x, ob_x, mod_x)
    if not last:
        yb_c = _diff_attention(lam_vecs, qt_c, k_c, p_c, None, None, sub_g, eye_v, lam_init, 512)
        cx = finish(cx, p_c, yb_c, of_c, ob_c, mod_c)
    return x, cx


def kernel(x, c, ctx, c_ctx, w_mod, b_mod, norm1_g, w_in, sgu_norm_g, sgu_w, sgu_b, w_a_br, q_norm_g, k_norm_g, lambda_q1, lambda_k1, lambda_q2, lambda_k2, subln_g, w_b_br, conv_qkv_w, a_log, dt_bias, gdn_norm_g, w_c_br, w_o, norm2_g, w_up, conv_ffn_w, w_down):
    bsz, n, d = x.shape
    depth = w_mod.shape[0]
    tables = _rope_tables(n)
    seg = jnp.arange(B_QK) // B_HEAD_DIM
    m_avg = jnp.where(seg[:, None] == seg[None, :], 1.0 / B_HEAD_DIM, 0.0).astype(BF16)
    eye_qk = jnp.eye(B_QK, dtype=BF16)
    rows = bsz + 1
    pad = (-rows) % 8
    cs = jnp.concatenate([c, c_ctx[None, :], jnp.zeros((pad, d), F32)], axis=0)
    cx = ctx
    for l in range(depth):
        mod = _modulation(cs, w_mod[l], b_mod[l].reshape(1, -1))
        lam_vecs = jnp.pad(jnp.stack([lambda_q1[l], lambda_k1[l], lambda_q2[l], lambda_k2[l]]).astype(F32),
                           ((0, 4), (0, LANES - B_HEAD_DIM)))
        prm = (norm1_g[l], w_in[l], sgu_norm_g[l], sgu_w[l], sgu_b[l], w_a_br[l], q_norm_g[l], k_norm_g[l],
               lam_vecs, subln_g[l], w_b_br[l], conv_qkv_w[l], a_log[l], dt_bias[l], gdn_norm_g[l],
               w_c_br[l], w_o[l], norm2_g[l], w_up[l], conv_ffn_w[l], w_down[l])
        x, cx = _layer(l, l == depth - 1, x, cx, mod, prm, tables, m_avg, eye_qk)
    return x
```

```python
import functools
import math

import jax
import jax.numpy as jnp
from jax import lax
from jax.experimental import pallas as pl
from jax.experimental.pallas import tpu as pltpu

F32 = jnp.float32
BF16 = jnp.bfloat16
HIGHEST = lax.Precision.HIGHEST

GRID_W = 64
A_WIDTH = 512
A_GROUPS = 4
A_GROUP_DIM = A_WIDTH // A_GROUPS
A_CHUNK = 128
B_HEADS = 4
B_HEAD_DIM = 64
B_QK = B_HEADS * 2 * B_HEAD_DIM
B_V = B_HEADS * 2 * B_HEAD_DIM
ROPE_BASE = 10000.0
C_HEADS = 4
C_HEAD_DIM = 128
C_WIDTH = C_HEADS * C_HEAD_DIM
GDN_CHUNK = 64
N_BRANCHES = 3
EPS = 1e-6

LANES = 128
BF16_SUBLANES = 16
VMEM_LIMIT = 52 * 1024 * 1024

BLK = 512
COL_GATE = 0
COL_CQKV = 6
COL_CZ = 9
COL_U = 10
COL_SV = 11
COL_BQ = 12
COL_BK = 13
COL_BV = 14
N_MAIN = 15 * BLK
GATE_TAB = 128


def _cparams(sem):
    return pltpu.CompilerParams(dimension_semantics=sem, vmem_limit_bytes=VMEM_LIMIT)


def _mm(a, b):
    return jnp.dot(a.astype(BF16), b.astype(BF16), preferred_element_type=F32)


def _silu(x):
    return x * jax.nn.sigmoid(x)


def _mod_kernel(c_ref, w_ref, b_ref, o_ref):
    s = _silu(c_ref[...])
    o_ref[...] = jnp.dot(s, w_ref[...], preferred_element_type=F32, precision=HIGHEST) + b_ref[...]


def _modulation(cs, w, b):
    r, d = cs.shape
    n = w.shape[1]
    tn = 1536
    return pl.pallas_call(
        _mod_kernel,
        grid=(n // tn,),
        in_specs=[pl.BlockSpec((r, d), lambda j: (0, 0)),
                  pl.BlockSpec((d, tn), lambda j: (0, j)),
                  pl.BlockSpec((1, tn), lambda j: (0, j))],
        out_specs=pl.BlockSpec((r, tn), lambda j: (0, j)),
        out_shape=jax.ShapeDtypeStruct((r, n), F32),
        compiler_params=_cparams(("arbitrary",)),
        name="modulation",
    )(cs, w, b)


def _nmm2_kernel(x_ref, g_ref, sh_ref, sc_ref, w_ref, w2_ref, o_ref, o2_ref, h_ref):
    @pl.when(pl.program_id(2) == 0)
    def _():
        x = x_ref[0]
        y = x * lax.rsqrt(jnp.mean(x * x, axis=-1, keepdims=True) + EPS) * g_ref[...]
        h_ref[...] = (y * (1.0 + sc_ref[0]) + sh_ref[0]).astype(BF16)
        o2_ref[0] = jnp.dot(h_ref[...], w2_ref[...], preferred_element_type=F32)

    o_ref[0] = jnp.dot(h_ref[...], w_ref[...], preferred_element_type=F32).astype(o_ref.dtype)


def _norm_mod_matmul2(x, g, shift, scale, w, w2, tm, tn, name):
    bsz, n, d = x.shape
    nc, nc2 = w.shape[1], w2.shape[1]
    tm = min(tm, n)
    return pl.pallas_call(
        _nmm2_kernel,
        grid=(bsz, n // tm, nc // tn),
        in_specs=[pl.BlockSpec((1, tm, d), lambda b, i, j: (b, i, 0)),
                  pl.BlockSpec((1, d), lambda b, i, j: (0, 0)),
                  pl.BlockSpec((1, 1, d), lambda b, i, j: (b, 0, 0)),
                  pl.BlockSpec((1, 1, d), lambda b, i, j: (b, 0, 0)),
                  pl.BlockSpec((d, tn), lambda b, i, j: (0, j)),
                  pl.BlockSpec((d, nc2), lambda b, i, j: (0, 0))],
        out_specs=[pl.BlockSpec((1, tm, tn), lambda b, i, j: (b, i, j)),
                   pl.BlockSpec((1, tm, nc2), lambda b, i, j: (b, i, 0))],
        out_shape=[jax.ShapeDtypeStruct((bsz, n, nc), BF16),
                   jax.ShapeDtypeStruct((bsz, n, nc2), F32)],
        scratch_shapes=[pltpu.VMEM((tm, d), BF16)],
        compiler_params=_cparams(("parallel", "parallel", "arbitrary")),
        name=name,
    )(x, g, shift, scale, w, w2)


def _sgu_kernel(u_ref, v_ref, ng_ref, w_ref, b_ref, o_ref):
    tm = u_ref.shape[1]
    for c in range(tm // A_CHUNK):
        rows = slice(c * A_CHUNK, (c + 1) * A_CHUNK)
        for g in range(A_GROUPS):
            cols = slice(g * A_GROUP_DIM, (g + 1) * A_GROUP_DIM)
            u = jax.nn.gelu(u_ref[0, rows, cols].astype(F32))
            v = jax.nn.gelu(v_ref[0, rows, cols].astype(F32))
            vn = v * lax.rsqrt(jnp.mean(v * v, axis=-1, keepdims=True) + EPS) * ng_ref[g:g + 1, :]
            mixed = _mm(w_ref[g], vn) + b_ref[:, g:g + 1]
            o_ref[0, rows, cols] = (u * mixed).astype(o_ref.dtype)


def _seg_rms(x, g, m_ref):
    x2 = x * x
    hi = x2.astype(BF16)
    lo = (x2 - hi.astype(F32)).astype(BF16)
    ms = (jnp.dot(hi, m_ref[...], preferred_element_type=F32)
          + jnp.dot(lo, m_ref[...], preferred_element_type=F32))
    return x * lax.rsqrt(ms + EPS) * g


def _rope(x, cos, sin_signed):
    half = B_HEAD_DIM // 4
    width = x.shape[1]
    lane = lax.broadcasted_iota(jnp.int32, x.shape, 1)
    first = (lane & (2 * half - 1)) < half
    partner = jnp.where(first, pltpu.roll(x, width - half, axis=1), pltpu.roll(x, half, axis=1))
    return x * cos + partner * sin_signed


def _transpose_store(x, eye_ref, o_ref):
    xt = lax.dot_general(eye_ref[...], x.astype(BF16), (((1,), (1,)), ((), ())), preferred_element_type=F32)
    o_ref[0] = xt.astype(o_ref.dtype)


Q_SCALE = B_HEAD_DIM ** -0.5 * math.log2(math.e)
ATTN_KEY_BLOCK = 512


def _qk_prep_rope_kernel(q_ref, k_ref, qg_ref, kg_ref, m_ref, eye_ref, cos_ref, sin_ref, ko_ref, qt_ref):
    cos = cos_ref[...]
    sin = sin_ref[...]
    q = _rope(_seg_rms(q_ref[0].astype(F32), qg_ref[...], m_ref), cos, sin)
    k = _rope(_seg_rms(k_ref[0].astype(F32), kg_ref[...], m_ref), cos, sin)
    ko_ref[0] = k.astype(ko_ref.dtype)
    _transpose_store(q * Q_SCALE, eye_ref, qt_ref)


def _qk_prep_plain_kernel(q_ref, k_ref, qg_ref, kg_ref, m_ref, eye_ref, ko_ref, qt_ref):
    q = _seg_rms(q_ref[0].astype(F32), qg_ref[...], m_ref)
    k = _seg_rms(k_ref[0].astype(F32), kg_ref[...], m_ref)
    ko_ref[0] = k.astype(ko_ref.dtype)
    _transpose_store(q * Q_SCALE, eye_ref, qt_ref)


def _qk_prep(p, qg, kg, m_avg, eye, tables, tm):
    bsz, n, _ = p.shape
    tm = min(tm, n)
    in_specs = [pl.BlockSpec((1, tm, BLK), lambda b, i: (b, i, COL_BQ)),
                pl.BlockSpec((1, tm, BLK), lambda b, i: (b, i, COL_BK)),
                pl.BlockSpec((1, B_QK), lambda b, i: (0, 0)),
                pl.BlockSpec((1, B_QK), lambda b, i: (0, 0)),
                pl.BlockSpec((B_QK, B_QK), lambda b, i: (0, 0)),
                pl.BlockSpec((B_QK, B_QK), lambda b, i: (0, 0))]
    args = [p, p, qg, kg, m_avg, eye]
    if tables is None:
        body = _qk_prep_plain_kernel
    else:
        body = _qk_prep_rope_kernel
        in_specs += [pl.BlockSpec((tm, B_QK), lambda b, i: (i, 0)),
                     pl.BlockSpec((tm, B_QK), lambda b, i: (i, 0))]
        args += list(tables)
    return pl.pallas_call(
        body,
        grid=(bsz, n // tm),
        in_specs=in_specs,
        out_specs=[pl.BlockSpec((1, tm, B_QK), lambda b, i: (b, i, 0)),
                   pl.BlockSpec((1, B_QK, tm), lambda b, i: (b, 0, i))],
        out_shape=[jax.ShapeDtypeStruct((bsz, n, B_QK), BF16),
                   jax.ShapeDtypeStruct((bsz, B_QK, n), BF16)],
        compiler_params=_cparams(("parallel", "parallel")),
        name="qk_prep",
    )(*args)


def _attn_kernel(*refs, has_lat, lam_init):
    if has_lat:
        lam_ref, qt_ref, kc_ref, vc_ref, kx_ref, vx_ref, g_ref, eye_ref, o_ref, vtc_s, vtx_s = refs
    else:
        lam_ref, qt_ref, kc_ref, vc_ref, g_ref, eye_ref, o_ref, vtc_s = refs
    hv = vc_ref.shape[2]

    @pl.when(pl.program_id(2) == 0)
    def _():
        pairs = ((vc_ref, vtc_s), (vx_ref, vtx_s)) if has_lat else ((vc_ref, vtc_s),)
        for v_ref, vt_s in pairs:
            vt = lax.dot_general(eye_ref[...], v_ref[0], (((1,), (1,)), ((), ())), preferred_element_type=F32)
            vt_s[:hv, :] = vt.astype(BF16)
            vt_s[hv:, :] = jnp.ones((BF16_SUBLANES, vt_s.shape[1]), BF16)

    qt = qt_ref[0]
    row = lax.broadcasted_iota(jnp.int32, qt.shape, 0)
    zero = jnp.zeros_like(qt)
    kblk = min(ATTN_KEY_BLOCK, kx_ref.shape[1]) if has_lat else 0
    assert not has_lat or kx_ref.shape[1] % kblk == 0
    n_blk = kx_ref.shape[1] // kblk if has_lat else 0
    blk = lambda j: slice((j - 1) * kblk, j * kblk)
    keys = lambda j: kc_ref[0] if j == 0 else kx_ref[0, blk(j), :]
    vals_t = lambda j: vtc_s[...] if j == 0 else vtx_s[:, blk(j)]
    q_sub = [jnp.where(row < B_HEAD_DIM, qt, zero), jnp.where(row >= B_HEAD_DIM, qt, zero)]
    score = lambda m, j: jnp.dot(keys(j), q_sub[m], preferred_element_type=F32)
    pv = lambda j, p: jnp.dot(vals_t(j), p, preferred_element_type=F32)

    m_run, acc = [None, None], [None, None]
    s_next = [score(m, 0) for m in range(2)]
    for j in range(n_blk + 1):
        for m in range(2):
            s = s_next[m]
            if j < n_blk:
                s_next[m] = score(m, j + 1)
            blk_max = jnp.max(s, axis=0, keepdims=True)
            if j == 0:
                m_new = blk_max
                acc[m] = pv(j, jnp.exp2(s - m_new).astype(BF16))
            else:
                m_new = jnp.maximum(m_run[m], blk_max)
                alpha = jnp.exp2(m_run[m] - m_new)
                acc[m] = alpha * acc[m] + pv(j, jnp.exp2(s - m_new).astype(BF16))
            m_run[m] = m_new
    outs = [a[:hv] / a[hv:hv + 1] for a in acc]
    lv = lam_ref[...]
    lam = (jnp.exp(jnp.sum(lv[0:1] * lv[1:2], axis=-1, keepdims=True))
           - jnp.exp(jnp.sum(lv[2:3] * lv[3:4], axis=-1, keepdims=True)) + lam_init)
    o = outs[0] - lam * outs[1]
    o = o * lax.rsqrt(jnp.mean(o * o, axis=0, keepdims=True) + EPS) * g_ref[...]
    o_ref[0] = (o * (1.0 - lam_init)).T.astype(o_ref.dtype)


def _diff_attention(lam_vecs, qt, kc, pc, kx, px, subln_g, eye, lam_init, tq):
    bsz, _, n = qt.shape
    nctx = kc.shape[1]
    tq = min(tq, n)
    hv = B_V // B_HEADS
    vcol = COL_BV * (BLK // hv)
    has_lat = kx is not None
    in_specs = [pl.BlockSpec(lam_vecs.shape, lambda b, h, i: (0, 0)),
                pl.BlockSpec((1, hv, tq), lambda b, h, i: (b, h, i)),
                pl.BlockSpec((1, nctx, hv), lambda b, h, i: (b, 0, h)),
                pl.BlockSpec((1, nctx, hv), lambda b, h, i: (b, 0, vcol + h))]
    args = [lam_vecs, qt, kc, pc]
    scratch = [pltpu.VMEM((hv + BF16_SUBLANES, nctx), BF16)]
    if has_lat:
        nlat = kx.shape[1]
        in_specs += [pl.BlockSpec((1, nlat, hv), lambda b, h, i: (b, 0, h)),
                     pl.BlockSpec((1, nlat, hv), lambda b, h, i: (b, 0, vcol + h))]
        args += [kx, px]
        scratch.append(pltpu.VMEM((hv + BF16_SUBLANES, nlat), BF16))
    in_specs += [pl.BlockSpec((hv, 1), lambda b, h, i: (0, 0)),
                 pl.BlockSpec((hv, hv), lambda b, h, i: (0, 0))]
    args += [subln_g, eye]
    return pl.pallas_call(
        functools.partial(_attn_kernel, has_lat=has_lat, lam_init=lam_init),
        grid=(bsz, B_HEADS, n // tq),
        in_specs=in_specs,
        out_specs=pl.BlockSpec((1, tq, hv), lambda b, h, i: (b, i, h)),
        out_shape=jax.ShapeDtypeStruct((bsz, n, B_V), BF16),
        scratch_shapes=scratch,
        compiler_params=_cparams(("parallel", "parallel", "arbitrary")),
        name="diff_attention",
    )(*args)


def _conv3(cur, prev_row, next_row, w):
    t = cur.shape[0]
    rid = lax.broadcasted_iota(jnp.int32, cur.shape, 0)
    xm = jnp.where(rid == 0, prev_row, pltpu.roll(cur, 1, axis=0))
    xp = jnp.where(rid == t - 1, next_row, pltpu.roll(cur, t - 1, axis=0))
    return w[0:1] * xm + w[1:2] * cur + w[2:3] * xp


def _halo_rows(prev_ref, next_ref, i, n_tiles):
    h = prev_ref.shape[1]
    prev_row = prev_ref[0, h - 1:h, :].astype(F32)
    next_row = next_ref[0, 0:1, :].astype(F32)
    prev_row = jnp.where(i > 0, prev_row, jnp.zeros_like(prev_row))
    next_row = jnp.where(i < n_tiles - 1, next_row, jnp.zeros_like(next_row))
    return prev_row, next_row


def _halo_specs(tm, width, col, n):
    r = tm // BF16_SUBLANES
    last = n // BF16_SUBLANES - 1
    cur = pl.BlockSpec((1, tm, width), lambda b, i, *_: (b, i, col(*_)))
    prev = pl.BlockSpec((1, BF16_SUBLANES, width),
                        lambda b, i, *_: (b, jnp.maximum(i * r - 1, 0), col(*_)))
    nxt = pl.BlockSpec((1, BF16_SUBLANES, width),
                       lambda b, i, *_: (b, jnp.minimum((i + 1) * r, last), col(*_)))
    return cur, prev, nxt


def _gdn_prep_kernel(x_ref, xp_ref, xn_ref, cw_ref, lg_ref, alog_ref, dtb_ref, eye_ref,
                     q_ref, k_ref, kt_ref, v_ref, tab_ref, row_ref):
    i = pl.program_id(1)
    tm = x_ref.shape[1]
    prev_row, next_row = _halo_rows(xp_ref, xn_ref, i, pl.num_programs(1))
    y = _silu(_conv3(x_ref[0].astype(F32), prev_row, next_row, cw_ref[...]))
    for h in range(C_HEADS):
        cq = slice(h * C_HEAD_DIM, (h + 1) * C_HEAD_DIM)
        ck = slice(C_WIDTH + h * C_HEAD_DIM, C_WIDTH + (h + 1) * C_HEAD_DIM)
        q = y[:, cq]
        k = y[:, ck]
        qn = q * lax.rsqrt(jnp.sum(q * q, axis=-1, keepdims=True) + EPS)
        kn = k * lax.rsqrt(jnp.sum(k * k, axis=-1, keepdims=True) + EPS)
        q_ref[0, :, cq] = (qn * C_HEAD_DIM ** -0.5).astype(q_ref.dtype)
        k_ref[0, :, cq] = kn.astype(k_ref.dtype)
    _transpose_store(k_ref[0], eye_ref, kt_ref)
    v_ref[0] = y[:, 2 * C_WIDTH:].astype(v_ref.dtype)

    lg = lg_ref[0]
    col = lax.broadcasted_iota(jnp.int32, lg.shape, 1)
    beta = jax.nn.sigmoid(lg)
    z = lg + dtb_ref[...]
    softplus = jnp.maximum(z, 0.0) + jnp.log(1.0 + jnp.exp(-jnp.abs(z)))
    g = -jnp.exp(alog_ref[...]) * softplus
    g = jnp.where((col >= 2 * C_HEADS) & (col < 4 * C_HEADS), g, 0.0)
    ri = lax.broadcasted_iota(jnp.int32, (tm, tm), 0)
    ci = lax.broadcasted_iota(jnp.int32, (tm, tm), 1)
    same = lax.shift_right_logical(ri, 6) == lax.shift_right_logical(ci, 6)
    lower = jnp.where(same & (ci <= ri), 1.0, 0.0).astype(F32)
    upper = jnp.where(same & (ci >= ri), 1.0, 0.0).astype(F32)
    cum_f = jnp.dot(lower, g, preferred_element_type=F32, precision=HIGHEST)
    cum_b = jnp.dot(upper, g, preferred_element_type=F32, precision=HIGHEST)
    cum = jnp.where(col >= 3 * C_HEADS, cum_b, cum_f)
    tab = jnp.where(col < 2 * C_HEADS, beta, cum)
    tab_ref[0] = tab
    sel = (lax.broadcasted_iota(jnp.int32, (4 * C_HEADS, GATE_TAB), 0)
           == lax.broadcasted_iota(jnp.int32, (4 * C_HEADS, GATE_TAB), 1)).astype(F32)
    row_ref[0] = lax.dot_general(sel, tab, (((1,), (1,)), ((), ())), preferred_element_type=F32,
                                 precision=HIGHEST)


def _gdn_prep(p, logits, conv_w, alog_row, dtb_row, eye, tm):
    bsz, n, _ = p.shape
    tm = min(tm, n)
    cur, prev, nxt = _halo_specs(tm, 3 * C_WIDTH, lambda: COL_CQKV // 3, n)
    bf = jax.ShapeDtypeStruct((bsz, n, C_WIDTH), BF16)
    tok = pl.BlockSpec((1, tm, C_WIDTH), lambda b, i: (b, i, 0))
    return pl.pallas_call(
        _gdn_prep_kernel,
        grid=(bsz, n // tm),
        in_specs=[cur, prev, nxt,
                  pl.BlockSpec(conv_w.shape, lambda b, i: (0, 0)),
                  pl.BlockSpec((1, tm, GATE_TAB), lambda b, i: (b, i, 0)),
                  pl.BlockSpec((1, GATE_TAB), lambda b, i: (0, 0)),
                  pl.BlockSpec((1, GATE_TAB), lambda b, i: (0, 0)),
                  pl.BlockSpec(eye.shape, lambda b, i: (0, 0))],
        out_specs=[tok, tok,
                   pl.BlockSpec((1, C_WIDTH, tm), lambda b, i: (b, 0, i)),
                   tok,
                   pl.BlockSpec((1, tm, GATE_TAB), lambda b, i: (b, i, 0)),
                   pl.BlockSpec((1, 4 * C_HEADS, tm), lambda b, i: (b, 0, i))],
        out_shape=[bf, bf, jax.ShapeDtypeStruct((bsz, C_WIDTH, n), BF16), bf,
                   jax.ShapeDtypeStruct((bsz, n, GATE_TAB), F32),
                   jax.ShapeDtypeStruct((bsz, 4 * C_HEADS, n), F32)],
        compiler_params=_cparams(("parallel", "parallel")),
        name="gdn_prep",
    )(p, p, p, conv_w, logits, alog_row, dtb_row, eye)


def _chunk_masks(d):
    ri = lax.broadcasted_iota(jnp.int32, (GDN_CHUNK, GDN_CHUNK), 0)
    ci = lax.broadcasted_iota(jnp.int32, (GDN_CHUNK, GDN_CHUNK), 1)
    if d == 0:
        return ri >= ci, ri > ci
    return ri <= ci, ri < ci


def _gate_views(tab, row, rows, d, h):
    beta = tab[rows, d * C_HEADS + h:d * C_HEADS + h + 1]
    gcol = tab[rows, 2 * C_HEADS + d * C_HEADS + h:2 * C_HEADS + d * C_HEADS + h + 1]
    grow = row[2 * C_HEADS + d * C_HEADS + h:2 * C_HEADS + d * C_HEADS + h + 1, rows]
    return beta, gcol, grow


def _gdn_local_kernel(k_ref, kt_ref, v_ref, tab_ref, row_ref, uf_ref, wf_ref, ub_ref, wb_ref):
    tl = k_ref.shape[1]
    tab = tab_ref[0]
    row = row_ref[0]
    nh = C_HEADS
    cw = nh * GDN_CHUNK
    r4 = lax.broadcasted_iota(jnp.int32, (cw, cw), 0)
    c4 = lax.broadcasted_iota(jnp.int32, (cw, cw), 1)
    bd_mask = lax.shift_right_logical(r4, 6) == lax.shift_right_logical(c4, 6)
    r64 = lax.broadcasted_iota(jnp.int32, (GDN_CHUNK, cw), 0)
    c64 = lax.broadcasted_iota(jnp.int32, (GDN_CHUNK, cw), 1) & (GDN_CHUNK - 1)
    eye_cat = jnp.where(r64 == c64, 1.0, 0.0)
    bd_zero = jnp.zeros((cw, cw), BF16)

    def block_diag(b):
        return jnp.where(bd_mask, jnp.concatenate([b.astype(BF16)] * nh, axis=0), bd_zero)

    def off_block(lg):
        shr = lax.shift_right_logical
        return (shr(r64, lg) != shr(c64, lg)) & (shr(r64, lg + 1) == shr(c64, lg + 1))

    out_refs = ((uf_ref, wf_ref), (ub_ref, wb_ref))
    units = [(c, d) for c in range(tl // GDN_CHUNK) for d in range(2)]
    a_cat, rhs = [], []
    for c, d in units:
        rows = slice(c * GDN_CHUNK, (c + 1) * GDN_CHUNK)
        _, strict = _chunk_masks(d)
        a_list, rhs_list = [], []
        for h in range(nh):
            cols = slice(h * C_HEAD_DIM, (h + 1) * C_HEAD_DIM)
            k = k_ref[0, rows, cols].astype(F32)
            v = v_ref[0, rows, cols].astype(F32)
            beta, gcol, grow = _gate_views(tab, row, rows, d, h)
            seg = jnp.exp(jnp.minimum(gcol - grow, 0.0))
            kb = k * beta
            a_list.append(jnp.where(strict, _mm(kb, kt_ref[0, cols, rows]) * seg, 0.0))
            rhs_list.append(jnp.concatenate([v * beta, kb * jnp.exp(gcol)], axis=1).astype(BF16))
        a_cat.append(jnp.concatenate(a_list, axis=1))
        rhs.append(rhs_list)
    t = [eye_cat - jnp.where(off_block(0), a, 0.0) for a in a_cat]
    for lg in range(1, 6):
        x = [_mm(jnp.where(off_block(lg), a, 0.0), block_diag(tu)) for a, tu in zip(a_cat, t)]
        t = [tu - _mm(tu, block_diag(xu)) for tu, xu in zip(t, x)]
    for (c, d), tu, rhs_list in zip(units, t, rhs):
        rows = slice(c * GDN_CHUNK, (c + 1) * GDN_CHUNK)
        u_ref, w_ref = out_refs[d]
        for h in range(nh):
            cols = slice(h * C_HEAD_DIM, (h + 1) * C_HEAD_DIM)
            uw = _mm(tu[:, h * GDN_CHUNK:(h + 1) * GDN_CHUNK], rhs_list[h])
            u_ref[0, rows, cols] = uw[:, :C_HEAD_DIM].astype(u_ref.dtype)
            w_ref[0, rows, cols] = uw[:, C_HEAD_DIM:].astype(w_ref.dtype)


def _gdn_local(k, kt, v, tab, row, tl):
    bsz, n, _ = k.shape
    tl = min(tl, n)
    tok = pl.BlockSpec((1, tl, C_WIDTH), lambda b, i: (b, i, 0))
    out = jax.ShapeDtypeStruct((bsz, n, C_WIDTH), BF16)
    return pl.pallas_call(
        _gdn_local_kernel,
        grid=(bsz, n // tl),
        in_specs=[tok,
                  pl.BlockSpec((1, C_WIDTH, tl), lambda b, i: (b, 0, i)),
                  tok,
                  pl.BlockSpec((1, tl, GATE_TAB), lambda b, i: (b, i, 0)),
                  pl.BlockSpec((1, 4 * C_HEADS, tl), lambda b, i: (b, 0, i))],
        out_specs=[tok, tok, tok, tok],
        out_shape=[out, out, out, out],
        compiler_params=_cparams(("parallel", "parallel")),
        name="gdn_local",
    )(k, kt, v, tab, row)


SCAN_TILE = 2 * GDN_CHUNK


def _gdn_scan_kernel(qf_ref, ktf_ref, uf_ref, wf_ref, tabf_ref, rowf_ref,
                     qb_ref, ktb_ref, ub_ref, wb_ref, tabb_ref, rowb_ref, s0_ref,
                     of_ref, ob_ref, sfin_ref, s_ref):
    step = pl.program_id(1)

    @pl.when(step == 0)
    def _():
        s_ref[...] = s0_ref[0]

    dirs = ((qf_ref, ktf_ref, uf_ref, wf_ref, tabf_ref, rowf_ref, of_ref),
            (qb_ref, ktb_ref, ub_ref, wb_ref, tabb_ref, rowb_ref, ob_ref))
    n_chunks = SCAN_TILE // GDN_CHUNK
    chains = [(d, h) for d in range(2) for h in range(C_HEADS)]
    tabs = [r[4][0] for r in dirs]
    rowtabs = [r[5][0] for r in dirs]
    for sub in range(n_chunks):
        local = []
        for d, h in chains:
            q_ref, kt_ref, u_ref, w_ref, _, _, _ = dirs[d]
            c = sub if d == 0 else n_chunks - 1 - sub
            rows = slice(c * GDN_CHUNK, (c + 1) * GDN_CHUNK)
            cols = slice(h * C_HEAD_DIM, (h + 1) * C_HEAD_DIM)
            incl, _ = _chunk_masks(d)
            q = q_ref[0, rows, cols].astype(F32)
            kt = kt_ref[0, cols, rows].astype(F32)
            _, gcol, grow = _gate_views(tabs[d], rowtabs[d], rows, d, h)
            glast = gcol[GDN_CHUNK - 1:GDN_CHUNK] if d == 0 else gcol[0:1]
            seg = jnp.where(incl, jnp.exp(jnp.minimum(gcol - grow, 0.0)), 0.0)
            intra = (_mm(q, kt) * seg).astype(BF16)
            qd = (q * jnp.exp(gcol)).astype(BF16)
            kt_tail = (kt * jnp.exp(glast - grow)).astype(BF16)
            lhs = jnp.concatenate([w_ref[0, rows, cols], qd], axis=0)
            local.append((rows, cols, lhs, intra, kt_tail, jnp.exp(glast)))
        states = [s_ref[d, h] for d, h in chains]
        ws_qs = [_mm(loc[2], s) for loc, s in zip(local, states)]
        for (d, h), loc, s, wq in zip(chains, local, states, ws_qs):
            rows, cols, _, intra, kt_tail, gtot = loc
            u_ref, o_ref = dirs[d][2], dirs[d][6]
            v_new = (u_ref[0, rows, cols].astype(F32) - wq[:GDN_CHUNK]).astype(BF16)
            o_ref[0, rows, cols] = wq[GDN_CHUNK:] + jnp.dot(intra, v_new, preferred_element_type=F32)
            s_ref[d, h] = s * gtot + jnp.dot(kt_tail, v_new, preferred_element_type=F32)

    @pl.when(step == pl.num_programs(1) - 1)
    def _():
        sfin_ref[0] = s_ref[...]


def _gdn_scan(q, kt, uf, wf, ub, wb, tab, row, s0):
    bsz, n, _ = q.shape
    nt = n // SCAN_TILE
    tok_f = pl.BlockSpec((1, SCAN_TILE, C_WIDTH), lambda b, i: (b, i, 0))
    tok_b = pl.BlockSpec((1, SCAN_TILE, C_WIDTH), lambda b, i: (b, nt - 1 - i, 0))
    ktr_f = pl.BlockSpec((1, C_WIDTH, SCAN_TILE), lambda b, i: (b, 0, i))
    ktr_b = pl.BlockSpec((1, C_WIDTH, SCAN_TILE), lambda b, i: (b, 0, nt - 1 - i))
    tab_f = pl.BlockSpec((1, SCAN_TILE, GATE_TAB), lambda b, i: (b, i, 0))
    tab_b = pl.BlockSpec((1, SCAN_TILE, GATE_TAB), lambda b, i: (b, nt - 1 - i, 0))
    row_f = pl.BlockSpec((1, 4 * C_HEADS, SCAN_TILE), lambda b, i: (b, 0, i))
    row_b = pl.BlockSpec((1, 4 * C_HEADS, SCAN_TILE), lambda b, i: (b, 0, nt - 1 - i))
    st = pl.BlockSpec((1, 2, C_HEADS, C_HEAD_DIM, C_HEAD_DIM), lambda b, i: (b, 0, 0, 0, 0))
    o_sds = jax.ShapeDtypeStruct((bsz, n, C_WIDTH), F32)
    return pl.pallas_call(
        _gdn_scan_kernel,
        grid=(bsz, nt),
        in_specs=[tok_f, ktr_f, tok_f, tok_f, tab_f, row_f,
                  tok_b, ktr_b, tok_b, tok_b, tab_b, row_b, st],
        out_specs=[tok_f, tok_b, st],
        out_shape=[o_sds, o_sds, jax.ShapeDtypeStruct(s0.shape, F32)],
        scratch_shapes=[pltpu.VMEM((2, C_HEADS, C_HEAD_DIM, C_HEAD_DIM), F32)],
        compiler_params=_cparams(("parallel", "arbitrary")),
        name="gdn_scan",
    )(q, kt, uf, wf, tab, row, q, kt, ub, wb, tab, row, s0)


def _merge_kernel(x_ref, g0_ref, g1_ref, g2_ref, u_ref, sv_ref, sng_ref, sw_ref, sb_ref, yb_ref, of_ref,
                  ob_ref, z_ref, gn_ref, gt_ref, wa_ref, wb_ref, wc_ref, wo_ref, o_ref, ya_ref):
    _sgu_kernel(u_ref, sv_ref, sng_ref, sw_ref, sb_ref, ya_ref)
    o = of_ref[0] + ob_ref[0]
    z = z_ref[0].astype(F32)
    parts = []
    for h in range(C_HEADS):
        cols = slice(h * C_HEAD_DIM, (h + 1) * C_HEAD_DIM)
        oh = o[:, cols]
        parts.append(oh * lax.rsqrt(jnp.mean(oh * oh, axis=-1, keepdims=True) + EPS))
    yc = jnp.concatenate(parts, axis=1) * gn_ref[...] * _silu(z)
    y = (jax.nn.sigmoid(g0_ref[0].astype(F32)) * jnp.dot(ya_ref[0], wa_ref[...], preferred_element_type=F32)
         + jax.nn.sigmoid(g1_ref[0].astype(F32)) * jnp.dot(yb_ref[0], wb_ref[...], preferred_element_type=F32)
         + jax.nn.sigmoid(g2_ref[0].astype(F32)) * _mm(yc, wc_ref[...]))
    o_ref[0] = x_ref[0] + gt_ref[0] * _mm(y, wo_ref[...])


def _merge(x, p, sgu_ng, sgu_w, sgu_bt, yb, of, ob, gn, gt, wa, wb, wc, wo, tm):
    bsz, n, d = x.shape
    tm = min(tm, n)
    tok = lambda width, col: pl.BlockSpec((1, tm, width), lambda b, i: (b, i, col))
    full = lambda a: pl.BlockSpec(a.shape, lambda b, i: (0,) * a.ndim)
    return pl.pallas_call(
        _merge_kernel,
        grid=(bsz, n // tm),
        in_specs=[tok(d, 0), tok(d, 0), tok(d, 1), tok(d, 2),
                  tok(BLK, COL_U), tok(BLK, COL_SV), full(sgu_ng), full(sgu_w), full(sgu_bt),
                  tok(B_V, 0), tok(C_WIDTH, 0), tok(C_WIDTH, 0), tok(BLK, COL_CZ),
                  full(gn), pl.BlockSpec((1, 1, d), lambda b, i: (b, 0, 0)),
                  full(wa), full(wb), full(wc), full(wo)],
        out_specs=tok(d, 0),
        out_shape=jax.ShapeDtypeStruct(x.shape, F32),
        scratch_shapes=[pltpu.VMEM((1, tm, A_WIDTH), BF16)],
        compiler_params=_cparams(("parallel", "parallel")),
        name="merge",
    )(x, p, p, p, p, p, sgu_ng, sgu_w, sgu_bt, yb, of, ob, p, gn, gt, wa, wb, wc, wo)


def _conv_ffn_kernel(x_ref, xp_ref, xn_ref, g_ref, sh_ref, sc_ref, gt_ref, wg_ref, wv_ref, cwg_ref,
                     cwv_ref, wd_ref, o_ref, h_ref, acc_ref):
    i = pl.program_id(1)
    kk = pl.program_id(2)
    nt = pl.num_programs(1)
    tm = x_ref.shape[1]
    hal = xp_ref.shape[1]

    @pl.when(kk == 0)
    def _():
        def norm_mod(x):
            y = x * lax.rsqrt(jnp.mean(x * x, axis=-1, keepdims=True) + EPS) * g_ref[...]
            return y * (1.0 + sc_ref[0]) + sh_ref[0]

        hp = norm_mod(xp_ref[0])
        hn = norm_mod(xn_ref[0])
        h_ref[0:hal, :] = jnp.where(i > 0, hp, jnp.zeros_like(hp)).astype(BF16)
        h_ref[hal:hal + tm, :] = norm_mod(x_ref[0]).astype(BF16)
        h_ref[hal + tm:, :] = jnp.where(i < nt - 1, hn, jnp.zeros_like(hn)).astype(BF16)
        acc_ref[...] = jnp.zeros(acc_ref.shape, F32)

    h = h_ref[...]

    def branch(w_ref, cw_ref):
        up = jnp.dot(h, w_ref[...], preferred_element_type=F32)
        rows = up.shape[0]
        cw = cw_ref[...]
        y = cw[0:1] * pltpu.roll(up, 1, axis=0) + cw[1:2] * up + cw[2:3] * pltpu.roll(up, rows - 1, axis=0)
        return y[hal:hal + tm]

    gate = branch(wg_ref, cwg_ref)
    val = branch(wv_ref, cwv_ref)
    acc_ref[...] += _mm(_silu(gate) * val, wd_ref[...])

    @pl.when(kk == pl.num_programs(2) - 1)
    def _():
        o_ref[0] = x_ref[0] + gt_ref[0] * acc_ref[...]


def _conv_ffn(x, g, shift, scale, gt, wu, conv_w, wd, tm, tk):
    bsz, n, d = x.shape
    f = wd.shape[0]
    tm = min(tm, n)
    nk = f // tk
    hal = BF16_SUBLANES
    x_cur, x_prev, x_next = _halo_specs(tm, d, lambda k: 0, n)
    vec = pl.BlockSpec((1, 1, d), lambda b, i, k: (b, 0, 0))
    return pl.pallas_call(
        _conv_ffn_kernel,
        grid=(bsz, n // tm, nk),
        in_specs=[x_cur, x_prev, x_next,
                  pl.BlockSpec((1, d), lambda b, i, k: (0, 0)),
                  vec, vec, vec,
                  pl.BlockSpec((d, tk), lambda b, i, k: (0, k)),
                  pl.BlockSpec((d, tk), lambda b, i, k: (0, nk + k)),
                  pl.BlockSpec((3, tk), lambda b, i, k: (0, k)),
                  pl.BlockSpec((3, tk), lambda b, i, k: (0, nk + k)),
                  pl.BlockSpec((tk, d), lambda b, i, k: (k, 0))],
        out_specs=pl.BlockSpec((1, tm, d), lambda b, i, k: (b, i, 0)),
        out_shape=jax.ShapeDtypeStruct(x.shape, F32),
        scratch_shapes=[pltpu.VMEM((tm + 2 * hal, d), BF16),
                        pltpu.VMEM((tm, d), F32)],
        compiler_params=_cparams(("parallel", "parallel", "arbitrary")),
        name="conv_ffn",
    )(x, x, x, g, shift, scale, gt, wu, wu, conv_w, conv_w, wd)


def _rope_tables(n):
    n_freq = B_HEAD_DIM // 4
    inv_freq = ROPE_BASE ** (-jnp.arange(n_freq, dtype=F32) / n_freq)
    pos = jnp.arange(n, dtype=jnp.int32)
    row = (pos // GRID_W).astype(F32)
    col = (pos % GRID_W).astype(F32)
    ang_r = row[:, None] * inv_freq
    ang_c = col[:, None] * inv_freq
    ang = jnp.concatenate([ang_r, ang_r, ang_c, ang_c], axis=-1)
    sign = jnp.tile(jnp.concatenate([-jnp.ones(n_freq, F32), jnp.ones(n_freq, F32)]), 2)
    reps = B_QK // B_HEAD_DIM
    return jnp.tile(jnp.cos(ang), (1, reps)), jnp.tile(jnp.sin(ang) * sign, (1, reps))


def _split_w_in(w):
    o_gate_logits = 2 * A_WIDTH + 2 * B_QK + B_V + 4 * C_WIDTH
    o_gates = o_gate_logits + 4 * C_HEADS
    o_c = 2 * A_WIDTH + 2 * B_QK + B_V
    main = jnp.concatenate([w[:, o_gates:], w[:, o_c:o_c + 4 * C_WIDTH], w[:, :o_c]], axis=1)
    logit = jnp.pad(w[:, o_gate_logits:o_gates], ((0, 0), (0, GATE_TAB - 4 * C_HEADS)))
    return main.astype(BF16), logit.astype(BF16)


def _gate_row(v):
    return jnp.pad(v.astype(F32).reshape(1, 2 * C_HEADS), ((0, 0), (2 * C_HEADS, GATE_TAB - 4 * C_HEADS)))


def _layer(l, last, x, cx, mod, prm, tables, m_avg, eye_qk):
    (norm1_g, w_in, sgu_norm_g, sgu_w, sgu_b, w_a_br, q_norm_g, k_norm_g, lam_vecs, subln_g, w_b_br,
     conv_qkv_w, a_log, dt_bias, gdn_norm_g, w_c_br, w_o, norm2_g, w_up, conv_ffn_w, w_down) = prm
    bsz, n, d = x.shape
    lam_init = 0.8 - 0.6 * math.exp(-0.3 * l)
    w_main, w_logit = _split_w_in(w_in)
    mod_x = mod[:bsz].reshape(bsz, 1, 6, d)
    mod_c = jnp.broadcast_to(mod[bsz:bsz + 1].reshape(1, 1, 6, d), (bsz, 1, 6, d))
    n1 = norm1_g.reshape(1, d)
    n2 = norm2_g.reshape(1, d)
    qg = jnp.tile(q_norm_g.astype(F32), B_QK // B_HEAD_DIM).reshape(1, B_QK)
    kg = jnp.tile(k_norm_g.astype(F32), B_QK // B_HEAD_DIM).reshape(1, B_QK)
    sub_g = subln_g.astype(F32).reshape(-1, 1)
    gn = jnp.tile(gdn_norm_g.astype(F32), C_HEADS).reshape(1, C_WIDTH)
    alog_row, dtb_row = _gate_row(a_log), _gate_row(dt_bias)
    sgu_wb = sgu_w.astype(BF16)
    sgu_bt = sgu_b.astype(F32).T
    wa, wb, wc, wo = (t.astype(BF16) for t in (w_a_br, w_b_br, w_c_br, w_o))
    wu, wd = w_up.astype(BF16), w_down.astype(BF16)

    def project(t, m):
        return _norm_mod_matmul2(t, n1, m[:, :, 0], m[:, :, 1], w_main, w_logit, 1024, 3840, "in_proj")

    def gdn_pre(p, lg):
        q, k, kt, v, tab, row = _gdn_prep(p, lg, conv_qkv_w.astype(F32), alog_row, dtb_row, eye_qk, 256)
        uf, wf, ub, wb_ = _gdn_local(k, kt, v, tab, row, 1024)
        return q, kt, uf, wf, ub, wb_, tab, row

    p_x, lg_x = project(x, mod_x)
    p_c, lg_c = project(cx, mod_c)

    eye_v = eye_qk[:B_V // B_HEADS, :B_V // B_HEADS]
    k_x, qt_x = _qk_prep(p_x, qg, kg, m_avg, eye_qk, tables, 512)
    k_c, qt_c = _qk_prep(p_c, qg, kg, m_avg, eye_qk, None, 512)
    yb_x = _diff_attention(lam_vecs, qt_x, k_c, p_c, k_x, p_x, sub_g, eye_v, lam_init, 2048)

    gx = gdn_pre(p_x, lg_x)
    gc = gdn_pre(p_c, lg_c)
    s0 = jnp.zeros((bsz, 2, C_HEADS, C_HEAD_DIM, C_HEAD_DIM), F32)
    of_c, ob_c, s_c = _gdn_scan(*gc, s0)
    of_x, ob_x, _ = _gdn_scan(*gx, s_c)

    def finish(t, p, yb, of, ob, m):
        t = _merge(t, p, sgu_norm_g.astype(F32), sgu_wb, sgu_bt, yb, of, ob, gn, m[:, :, 2], wa, wb, wc, wo, 512)
        return _conv_ffn(t, n2, m[:, :, 3], m[:, :, 4], m[:, :, 5], wu, conv_ffn_w.astype(F32), wd, 1024, 1408)

    x = finish(x, p_x, yb_x, of_x, ob_x, mod_x)
    if not last:
        yb_c = _diff_attention(lam_vecs, qt_c, k_c, p_c, None, None, sub_g, eye_v, lam_init, 512)
        cx = finish(cx, p_c, yb_c, of_c, ob_c, mod_c)
    return x, cx


def kernel(x, c, ctx, c_ctx, w_mod, b_mod, norm1_g, w_in, sgu_norm_g, sgu_w, sgu_b, w_a_br, q_norm_g, k_norm_g, lambda_q1, lambda_k1, lambda_q2, lambda_k2, subln_g, w_b_br, conv_qkv_w, a_log, dt_bias, gdn_norm_g, w_c_br, w_o, norm2_g, w_up, conv_ffn_w, w_down):
    bsz, n, d = x.shape
    depth = w_mod.shape[0]
    tables = _rope_tables(n)
    seg = jnp.arange(B_QK) // B_HEAD_DIM
    m_avg = jnp.where(seg[:, None] == seg[None, :], 1.0 / B_HEAD_DIM, 0.0).astype(BF16)
    eye_qk = jnp.eye(B_QK, dtype=BF16)
    rows = bsz + 1
    pad = (-rows) % 8
    cs = jnp.concatenate([c, c_ctx[None, :], jnp.zeros((pad, d), F32)], axis=0)
    cx = ctx
    for l in range(depth):
        mod = _modulation(cs, w_mod[l], b_mod[l].reshape(1, -1))
        lam_vecs = jnp.pad(jnp.stack([lambda_q1[l], lambda_k1[l], lambda_q2[l], lambda_k2[l]]).astype(F32),
                           ((0, 4), (0, LANES - B_HEAD_DIM)))
        prm = (norm1_g[l], w_in[l], sgu_norm_g[l], sgu_w[l], sgu_b[l], w_a_br[l], q_norm_g[l], k_norm_g[l],
               lam_vecs, subln_g[l], w_b_br[l], conv_qkv_w[l], a_log[l], dt_bias[l], gdn_norm_g[l],
               w_c_br[l], w_o[l], norm2_g[l], w_up[l], conv_ffn_w[l], w_down[l])
        x, cx = _layer(l, l == depth - 1, x, cx, mod, prm, tables, m_avg, eye_qk)
    return x
```
